```python
import math
import jax, jax.numpy as jnp
from jax import lax
import numpy as np

D_MODEL = 4096
BATCH = 4
SEQ = 2048
DEPTH = 4
DEC_BATCH = 8
DEC_SEQ = 4
PAST_LEN = 8192
PAGE_SIZE = 128

HEAD_DIM = 128
N_HEADS = D_MODEL // HEAD_DIM
H_DIFF = N_HEADS // 2
H_FOX = N_HEADS - H_DIFF
W_DIFF = H_DIFF * HEAD_DIM
W_FOX = H_FOX * HEAD_DIM
DH_HALF = HEAD_DIM // 2
D_FF = 4 * D_MODEL
N_BUCKETS = 32
MAX_DISTANCE = 128
Q_BLOCK = 128
EPS = 1e-6
N_MOD = 6
NEG_INF = -1e30
OFF_QA = 0
OFF_KA = OFF_QA + W_DIFF
OFF_VA = OFF_KA + W_DIFF
OFF_QF = OFF_VA + W_DIFF
OFF_KF = OFF_QF + W_FOX
OFF_VF = OFF_KF + W_FOX
OFF_F = OFF_VF + W_FOX
IN_COLS = OFF_F + H_FOX

kernel_name = "hybrid_diff_fox_adaln_decoder_step"


def rms_norm(x, g):
    xf = x.astype(jnp.float32)
    y = xf * lax.rsqrt(jnp.mean(xf * xf, axis=-1, keepdims=True) + EPS)
    return (y * g.astype(jnp.float32)).astype(x.dtype)


def t5_bucket(rel):
    n = jnp.maximum(rel, 0)
    max_exact = N_BUCKETS // 2
    nf = jnp.maximum(n, 1).astype(jnp.float32)
    large = max_exact + (jnp.log(nf / max_exact) / math.log(MAX_DISTANCE / max_exact)
                         * (N_BUCKETS - max_exact)).astype(jnp.int32)
    large = jnp.minimum(large, N_BUCKETS - 1)
    return jnp.where(n < max_exact, n, large)


def _q_block(t):
    return t if t <= Q_BLOCK else Q_BLOCK


def _split_blocks(a, nb, blk):
    return jnp.moveaxis(a.reshape(a.shape[0], nb, blk, *a.shape[2:]), 1, 0)


def _merge_blocks(a):
    a = jnp.moveaxis(a, 0, 1)
    return a.reshape(a.shape[0], a.shape[1] * a.shape[2], *a.shape[3:])


def diff_attention(q, k, v, q_pos, k_pos, lam, rel_bias):
    t = q.shape[1]
    blk = _q_block(t)
    nb = t // blk
    scale = DH_HALF ** -0.5
    k1, k2 = k[..., :DH_HALF], k[..., DH_HALF:]

    def one_block(args):
        qb, pb = args
        rel = pb[:, None] - k_pos[None, :]
        visible = rel >= 0
        bias = jnp.transpose(rel_bias[t5_bucket(rel)], (2, 0, 1)).astype(jnp.float32)

        def probs(qh, kh):
            s = jnp.einsum('bqhd,bkhd->bhqk', qh, kh,
                           preferred_element_type=jnp.float32) * scale + bias
            return jax.nn.softmax(jnp.where(visible, s, NEG_INF), axis=-1)

        a = probs(qb[..., :DH_HALF], k1) - lam * probs(qb[..., DH_HALF:], k2)
        return jnp.einsum('bhqk,bkhd->bqhd', a.astype(v.dtype), v)

    out = lax.map(one_block, (_split_blocks(q, nb, blk), q_pos.reshape(nb, blk)))
    return _merge_blocks(out)


def fox_attention(q, k, v, f_q, f_k, q_pos, k_pos):
    t = q.shape[1]
    blk = _q_block(t)
    nb = t // blk
    scale = HEAD_DIM ** -0.5
    fk_t = jnp.transpose(f_k, (0, 2, 1))[:, :, None, :]

    def one_block(args):
        qb, fqb, pb = args
        visible = pb[:, None] >= k_pos[None, :]
        decay = jnp.transpose(fqb, (0, 2, 1))[..., None] - fk_t
        s = jnp.einsum('bqhd,bkhd->bhqk', qb, k,
                       preferred_element_type=jnp.float32) * scale + decay
        p = jax.nn.softmax(jnp.where(visible, s, NEG_INF), axis=-1)
        return jnp.einsum('bhqk,bkhd->bqhd', p.astype(v.dtype), v)

    out = lax.map(one_block, (_split_blocks(q, nb, blk), _split_blocks(f_q, nb, blk),
                              q_pos.reshape(nb, blk)))
    return _merge_blocks(out)


def gather_pages(cache, l, page_table):
    g = cache[l, page_table]
    return g.reshape(g.shape[0], g.shape[1] * g.shape[2], *g.shape[3:])


def run_trunk(x, c, past, q_pos, k_pos, w_ada, b_ada, ada_table, norm_attn, norm_mlp, norm_final,
              w_in, b_forget, diff_lambda, subln_gain, rel_bias, w_out, w_up, w_down):
    bsz, t, _ = x.shape
    mod = (jax.nn.silu(c) @ w_ada + b_ada).reshape(bsz, N_MOD, D_MODEL)
    rows = ([], [], [], [], [])
    for l in range(DEPTH):
        m = mod + ada_table[l][None]
        shift_a, scale_a, gate_a, shift_m, scale_m, gate_m = [m[:, i, None, :] for i in range(N_MOD)]
        h = rms_norm(x, norm_attn[l]) * (1 + scale_a) + shift_a
        p = h @ w_in[l]
        qa = p[..., OFF_QA:OFF_KA].reshape(bsz, t, H_DIFF, HEAD_DIM)
        ka = p[..., OFF_KA:OFF_VA].reshape(bsz, t, H_DIFF, HEAD_DIM)
        va = p[..., OFF_VA:OFF_QF].reshape(bsz, t, H_DIFF, HEAD_DIM)
        qf = p[..., OFF_QF:OFF_KF].reshape(bsz, t, H_FOX, HEAD_DIM)
        kf = p[..., OFF_KF:OFF_VF].reshape(bsz, t, H_FOX, HEAD_DIM)
        vf = p[..., OFF_VF:OFF_F].reshape(bsz, t, H_FOX, HEAD_DIM)
        logf = jax.nn.log_sigmoid(p[..., OFF_F:IN_COLS].astype(jnp.float32)
                                  + b_forget[l].astype(jnp.float32))
        if past is None:
            ka_all, va_all, kf_all, vf_all = ka, va, kf, vf
            f_q = jnp.cumsum(logf, axis=1)
            f_k = f_q
        else:
            page_table, c_ka, c_va, c_kf, c_vf, c_lf = past
            ka_all = jnp.concatenate([gather_pages(c_ka, l, page_table).astype(ka.dtype), ka], axis=1)
            va_all = jnp.concatenate([gather_pages(c_va, l, page_table).astype(va.dtype), va], axis=1)
            kf_all = jnp.concatenate([gather_pages(c_kf, l, page_table).astype(kf.dtype), kf], axis=1)
            vf_all = jnp.concatenate([gather_pages(c_vf, l, page_table).astype(vf.dtype), vf], axis=1)
            lf_past = gather_pages(c_lf, l, page_table).astype(jnp.float32)
            f_past = lf_past - lax.cumsum(lf_past, axis=1, reverse=True)
            f_q = jnp.cumsum(logf, axis=1)
            f_k = jnp.concatenate([f_past, f_q], axis=1)
        lam_init = 0.8 - 0.6 * math.exp(-0.3 * l)
        dl = diff_lambda[l].astype(jnp.float32)
        lam = jnp.exp(jnp.sum(dl[0] * dl[1])) - jnp.exp(jnp.sum(dl[2] * dl[3])) + lam_init
        oa = diff_attention(qa, ka_all, va_all, q_pos, k_pos, lam, rel_bias)
        oa = rms_norm(oa, subln_gain[l]) * (1 - lam_init)
        of = fox_attention(qf, kf_all, vf_all, f_q, f_k, q_pos, k_pos)
        o = jnp.concatenate([oa.reshape(bsz, t, W_DIFF), of.reshape(bsz, t, W_FOX)], axis=-1) @ w_out[l]
        x = x + gate_a * o
        h2 = rms_norm(x, norm_mlp[l]) * (1 + scale_m) + shift_m
        x = x + gate_m * (jnp.square(jax.nn.relu(h2 @ w_up[l])) @ w_down[l])
        for lst, r in zip(rows, (ka, va, kf, vf, logf)):
            lst.append(r)
    y = rms_norm(x, norm_final)
    return y, [jnp.stack(lst, axis=0) for lst in rows]


def setup_inputs(seed: int = 0) -> dict:
    key = jax.random.key(seed)
    ks = jax.random.split(key, 32)
    f32 = jnp.float32
    n_pages = PAST_LEN // PAGE_SIZE
    n_pool = (DEC_BATCH * n_pages * 5) // 4
    nrm = lambda k, s, sc: jax.random.normal(k, s, f32) * sc
    kv_shape = (DEPTH, n_pool, PAGE_SIZE, H_DIFF, HEAD_DIM)
    page_table = jax.random.permutation(ks[10], n_pool)[:DEC_BATCH * n_pages]
    page_table = page_table.reshape(DEC_BATCH, n_pages).astype(jnp.int32)
    return {
        "x_prompt": nrm(ks[0], (BATCH, SEQ, D_MODEL), 1.0),
        "x_sample": nrm(ks[1], (DEC_BATCH, DEC_SEQ, D_MODEL), 1.0),
        "cache_k_diff": nrm(ks[2], kv_shape, 1.0),
        "cache_v_diff": nrm(ks[3], kv_shape, 1.0),
        "cache_k_fox": nrm(ks[4], (DEPTH, n_pool, PAGE_SIZE, H_FOX, HEAD_DIM), 1.0),
        "cache_v_fox": nrm(ks[5], (DEPTH, n_pool, PAGE_SIZE, H_FOX, HEAD_DIM), 1.0),
        "cache_logf_fox": jax.nn.log_sigmoid(3.0 + nrm(ks[6], (DEPTH, n_pool, PAGE_SIZE, H_FOX), 1.0)),
        "page_table": page_table,
        "c_prompt": nrm(ks[7], (BATCH, D_MODEL), 1.0),
        "c_sample": nrm(ks[8], (DEC_BATCH, D_MODEL), 1.0),
        "w_ada": nrm(ks[11], (D_MODEL, N_MOD * D_MODEL), 0.5 * D_MODEL ** -0.5),
        "b_ada": nrm(ks[12], (N_MOD * D_MODEL,), 0.01),
        "ada_table": nrm(ks[13], (DEPTH, N_MOD, D_MODEL), 0.1),
        "norm_attn": 1.0 + nrm(ks[14], (DEPTH, D_MODEL), 0.05),
        "norm_mlp": 1.0 + nrm(ks[15], (DEPTH, D_MODEL), 0.05),
        "norm_final": 1.0 + nrm(ks[16], (D_MODEL,), 0.05),
        "w_in": nrm(ks[17], (DEPTH, D_MODEL, IN_COLS), D_MODEL ** -0.5),
        "b_forget": 3.0 + nrm(ks[18], (DEPTH, H_FOX), 0.5),
        "diff_lambda": nrm(ks[19], (DEPTH, 4, DH_HALF), 0.1),
        "subln_gain": 1.0 + nrm(ks[20], (DEPTH, HEAD_DIM), 0.05),
        "rel_bias": nrm(ks[21], (N_BUCKETS, H_DIFF), 0.5),
        "w_out": nrm(ks[22], (DEPTH, W_DIFF + W_FOX, D_MODEL), (W_DIFF + W_FOX) ** -0.5),
        "w_up": nrm(ks[23], (DEPTH, D_MODEL, D_FF), D_MODEL ** -0.5),
        "w_down": nrm(ks[24], (DEPTH, D_FF, D_MODEL), D_FF ** -0.5),
    }


def reference(x_prompt, x_sample, cache_k_diff, cache_v_diff, cache_k_fox, cache_v_fox, cache_logf_fox,
              page_table, c_prompt, c_sample, w_ada, b_ada, ada_table, norm_attn, norm_mlp, norm_final,
              w_in, b_forget, diff_lambda, subln_gain, rel_bias, w_out, w_up, w_down):
    weights = (w_ada, b_ada, ada_table, norm_attn, norm_mlp, norm_final, w_in, b_forget,
               diff_lambda, subln_gain, rel_bias, w_out, w_up, w_down)
    pos_p = jnp.arange(x_prompt.shape[1], dtype=jnp.int32)
    y_prompt, rp = run_trunk(x_prompt, c_prompt, None, pos_p, pos_p, *weights)
    past_len = page_table.shape[1] * cache_k_diff.shape[2]
    t_s = x_sample.shape[1]
    q_pos_s = past_len + jnp.arange(t_s, dtype=jnp.int32)
    k_pos_s = jnp.arange(past_len + t_s, dtype=jnp.int32)
    past = (page_table, cache_k_diff, cache_v_diff, cache_k_fox, cache_v_fox, cache_logf_fox)
    y_sample, rs = run_trunk(x_sample, c_sample, past, q_pos_s, k_pos_s, *weights)
    return (y_prompt, y_sample, rp[0], rp[1], rp[2], rp[3], rp[4], rs[0], rs[1], rs[2], rs[3], rs[4])
```

```python
import functools
import math

import numpy as np
import jax
import jax.numpy as jnp
from jax import lax
from jax.experimental import pallas as pl
from jax.experimental.pallas import tpu as pltpu

HEAD_DIM = 128
DH_HALF = HEAD_DIM // 2
N_BUCKETS = 32
MAX_DISTANCE = 128
EPS = 1e-6
N_MOD = 6
NEG_INF = -1e30
LANES = 128
SUBLANES = 8
VMEM_LIMIT = 56 * 1024 * 1024
PAGES_PER_STEP = 4
PAGES_PER_SUM_STEP = 16
F32 = jnp.float32
BF16 = jnp.bfloat16

_NT = (((1,), (1,)), ((), ()))


def _params(n_axes):
    return pltpu.CompilerParams(dimension_semantics=("arbitrary",) * n_axes,
                                vmem_limit_bytes=VMEM_LIMIT)


def _tile(dim, pref, align):
    t = (min(pref, dim) // align) * align
    while t >= align:
        if dim % t == 0:
            return t
        t -= align
    return dim


def _split3(x):
    hi = x.astype(BF16)
    r1 = x - hi.astype(F32)
    mid = r1.astype(BF16)
    lo = (r1 - mid.astype(F32)).astype(BF16)
    return hi, mid, lo


def _dot(a, b):
    return jnp.dot(a, b, preferred_element_type=F32)


def _dot_nt(a, b):
    return lax.dot_general(a, b, _NT, preferred_element_type=F32)


def _dot3_rhs(mat, x):
    hi, mid, lo = _split3(x)
    return _dot(mat, hi) + _dot(mat, mid) + _dot(mat, lo)


def _dot3_lhs(x, mat):
    hi, mid, lo = _split3(x)
    return _dot(hi, mat) + _dot(mid, mat) + _dot(lo, mat)


def _log_sigmoid(z):
    return jnp.minimum(z, 0.0) - jnp.log1p(jnp.exp(-jnp.abs(z)))


def _mm_kernel(*refs, nk, tm, silu, scale, epi, want_main, want_hm):
    a_ref, w_ref = refs[0], refs[1]
    pos = 2
    bias_ref = gate_ref = resid_ref = main_ref = hm_ref = None
    if epi == "bias":
        bias_ref = refs[pos]
        pos += 1
    if epi == "resid":
        gate_ref, resid_ref = refs[pos], refs[pos + 1]
        pos += 2
    if want_main:
        main_ref = refs[pos]
        pos += 1
    if want_hm:
        hm_ref = refs[pos]
        pos += 1
    wbf_ref = refs[pos]
    k = pl.program_id(2)
    m = pl.program_id(3)

    @pl.when(m == 0)
    def _():
        wbf_ref[...] = w_ref[...].astype(BF16)

    a = a_ref[...]
    if silu:
        a = (a * jax.nn.sigmoid(a)).astype(BF16)
    acc = _dot(a, wbf_ref[...])

    if nk == 1:
        if epi == "bias":
            acc = acc + bias_ref[...]
        elif epi == "relu2":
            r = jnp.maximum(acc, 0.0)
            acc = r * r
        elif epi == "resid":
            acc = resid_ref[...] + gate_ref[0] * acc
        if want_main:
            main_ref[...] = acc.astype(main_ref.dtype)
        if want_hm:
            for hh in range(hm_ref.shape[0]):
                piece = acc[:, hh * HEAD_DIM:(hh + 1) * HEAD_DIM]
                hm_ref[hh] = (piece * scale).astype(hm_ref.dtype)
    else:
        rows = pl.ds(pl.multiple_of(m * tm, tm), tm)

        @pl.when(k == 0)
        def _():
            main_ref[rows, :] = acc

        @pl.when(jnp.logical_and(k > 0, k < nk - 1))
        def _():
            main_ref[rows, :] += acc

        @pl.when(k == nk - 1)
        def _():
            main_ref[rows, :] = resid_ref[...] + gate_ref[0] * (main_ref[rows, :] + acc)


def _matmul(a, w, layer, n0, n, *, groups=1, tm=512, tn=512, tk=None, silu=False, scale=1.0,
            epi="none", bias=None, gate=None, resid=None, main_dtype=None, want_hm=False):
    M, K = a.shape
    mg = M // groups
    tm = _tile(mg, tm, SUBLANES)
    tn = _tile(n, tn, LANES)
    tk = K if tk is None else _tile(K, tk, LANES)
    assert n0 % tn == 0 and M % groups == 0
    nm, nj, nk = mg // tm, n // tn, K // tk
    jb0 = n0 // tn
    want_main = main_dtype is not None
    assert nk == 1 or (epi == "resid" and want_main and not want_hm)

    in_specs = [
        pl.BlockSpec((tm, tk), lambda g, j, k, m: (g * nm + m, k)),
        pl.BlockSpec((None, tk, tn), lambda g, j, k, m: (layer, k, jb0 + j)),
    ]
    args = [a, w]
    if epi == "bias":
        in_specs.append(pl.BlockSpec((1, tn), lambda g, j, k, m: (0, j)))
        args.append(bias)
    if epi == "resid":
        rb = gate.shape[1]
        assert rb in (1, tm)
        in_specs.append(pl.BlockSpec((1, rb, tn), lambda g, j, k, m: (g, 0, j)))
        if nk == 1:
            in_specs.append(pl.BlockSpec((tm, tn), lambda g, j, k, m: (g * nm + m, j)))
        else:
            in_specs.append(pl.BlockSpec(
                (tm, tn), lambda g, j, k, m: (g * nm + jnp.where(k == nk - 1, m, 0), j)))
        args += [gate, resid]

    out_shape, out_specs = [], []
    if want_main:
        out_shape.append(jax.ShapeDtypeStruct((M, n), main_dtype))
        if nk == 1:
            out_specs.append(pl.BlockSpec((tm, tn), lambda g, j, k, m: (g * nm + m, j)))
        else:
            out_specs.append(pl.BlockSpec((mg, tn), lambda g, j, k, m: (g, j)))
    if want_hm:
        hpt = tn // HEAD_DIM
        out_shape.append(jax.ShapeDtypeStruct((n // HEAD_DIM, M, HEAD_DIM), BF16))
        out_specs.append(pl.BlockSpec((hpt, tm, HEAD_DIM), lambda g, j, k, m: (j, g * nm + m, 0)))

    kern = functools.partial(_mm_kernel, nk=nk, tm=tm, silu=silu, scale=scale, epi=epi,
                             want_main=want_main, want_hm=want_hm)
    outs = pl.pallas_call(
        kern,
        grid=(groups, nj, nk, nm),
        in_specs=in_specs,
        out_specs=out_specs,
        out_shape=out_shape,
        scratch_shapes=[pltpu.VMEM((tk, tn), BF16)],
        compiler_params=_params(4),
        name="matmul_" + epi,
    )(*args)
    return outs[0] if len(outs) == 1 else tuple(outs)


def _norm_kernel(*refs, modulate):
    if modulate:
        x_ref, g_ref, sc_ref, sh_ref, o_ref = refs
    else:
        x_ref, g_ref, o_ref = refs
    x = x_ref[...]
    y = x * lax.rsqrt(jnp.mean(x * x, axis=-1, keepdims=True) + EPS) * g_ref[...]
    if modulate:
        y = y * (1.0 + sc_ref[0]) + sh_ref[0]
    o_ref[...] = y.astype(o_ref.dtype)


def _norm(x, g, scale=None, shift=None, *, groups=1, out_dtype=BF16, tm=256):
    M, D = x.shape
    mg = M // groups
    tm = _tile(mg, tm, SUBLANES)
    nm = mg // tm
    modulate = scale is not None
    in_specs = [pl.BlockSpec((tm, D), lambda i: (i, 0)), pl.BlockSpec((1, D), lambda i: (0, 0))]
    args = [x, g.reshape(1, D)]
    if modulate:
        rb = scale.shape[1]
        assert rb in (1, tm)
        spec = pl.BlockSpec((1, rb, D), lambda i: (i // nm, 0, 0))
        in_specs += [spec, spec]
        args += [scale, shift]
    return pl.pallas_call(
        functools.partial(_norm_kernel, modulate=modulate),
        grid=(M // tm,),
        in_specs=in_specs,
        out_specs=pl.BlockSpec((tm, D), lambda i: (i, 0)),
        out_shape=jax.ShapeDtypeStruct((M, D), out_dtype),
        compiler_params=_params(1),
        name="rmsnorm",
    )(*args)


def _forget_kernel(h_ref, wf_ref, wft_ref, brow_ref, bcol_ref, logf_ref, cumt_ref, carry_ref,
                   *, tm, seq):
    i = pl.program_id(0)
    h = h_ref[...]
    z = _dot(h, wf_ref[...].astype(BF16)) + brow_ref[...]
    logf_ref[...] = _log_sigmoid(z)
    zt = _dot_nt(wft_ref[...].astype(BF16), h)
    lt = _log_sigmoid(zt + bcol_ref[...])

    src = lax.broadcasted_iota(jnp.int32, (tm, tm), 0)
    dst = lax.broadcasted_iota(jnp.int32, (tm, tm), 1)
    tri = src <= dst
    if tm > seq:
        tri = jnp.logical_and(tri, src // seq == dst // seq)
    tri = jnp.where(tri, 1.0, 0.0).astype(BF16)
    cs = _dot3_lhs(lt, tri)
    if tm < seq:
        @pl.when((i * tm) % seq == 0)
        def _():
            carry_ref[...] = jnp.zeros_like(carry_ref)

        cs = cs + carry_ref[...]
        carry_ref[...] = cs[:, tm - 1:tm]
    cumt_ref[...] = cs


def _forget(h, wf, b, seq):
    M, D = h.shape
    H = wf.shape[1]
    tm = _tile(M, 256, LANES) if M >= LANES else M
    assert seq % tm == 0 or tm % seq == 0
    return pl.pallas_call(
        functools.partial(_forget_kernel, tm=tm, seq=seq),
        grid=(M // tm,),
        in_specs=[
            pl.BlockSpec((tm, D), lambda i: (i, 0)),
            pl.BlockSpec((D, H), lambda i: (0, 0)),
            pl.BlockSpec((H, D), lambda i: (0, 0)),
            pl.BlockSpec((1, H), lambda i: (0, 0)),
            pl.BlockSpec((H, 1), lambda i: (0, 0)),
        ],
        out_specs=[pl.BlockSpec((tm, H), lambda i: (i, 0)), pl.BlockSpec((H, tm), lambda i: (0, i))],
        out_shape=[jax.ShapeDtypeStruct((M, H), F32), jax.ShapeDtypeStruct((H, M), F32)],
        scratch_shapes=[pltpu.VMEM((H, 1), F32)],
        compiler_params=_params(1),
        name="forget_gate",
    )(h, wf, wf.T, b.reshape(1, H), b.reshape(H, 1))


def _bucket_thresholds():
    n = np.arange(0, 4 * MAX_DISTANCE, dtype=np.int32)
    max_exact = N_BUCKETS // 2
    nf = np.maximum(n, 1).astype(np.float32)
    large = max_exact + (np.log(nf / np.float32(max_exact)) / np.float32(math.log(MAX_DISTANCE / max_exact))
                         * np.float32(N_BUCKETS - max_exact)).astype(np.int32)
    bucket = np.where(n < max_exact, n, np.minimum(large, N_BUCKETS - 1))
    assert np.all(np.diff(bucket) >= 0) and bucket[-1] == N_BUCKETS - 1
    return [int(np.argmax(bucket >= t)) for t in range(N_BUCKETS)]


def _rel_bias_kernel(rb_ref, o_ref, *, rel0s, thr):
    h = pl.program_id(0)
    _, _, R, C = o_ref.shape
    d = lax.broadcasted_iota(jnp.int32, (R, C), 0) - lax.broadcasted_iota(jnp.int32, (R, C), 1)
    for p, rel0 in enumerate(rel0s):
        rel = d + rel0
        acc = jnp.full((R, C), rb_ref[0, h], F32)
        for t in range(1, N_BUCKETS):
            acc = jnp.where(rel >= thr[t], rb_ref[t, h], acc)
        o_ref[0, p] = acc


def _rel_bias_tiles(rel_bias, R, C, rel0s):
    H = rel_bias.shape[1]
    return pl.pallas_call(
        functools.partial(_rel_bias_kernel, rel0s=tuple(rel0s), thr=_bucket_thresholds()),
        grid=(H,),
        in_specs=[pl.BlockSpec(memory_space=pltpu.SMEM)],
        out_specs=pl.BlockSpec((1, len(rel0s), R, C), lambda h: (h, 0, 0, 0)),
        out_shape=jax.ShapeDtypeStruct((H, len(rel0s), R, C), F32),
        compiler_params=_params(1),
        name="rel_bias_tiles",
    )(rel_bias)


def _lambda(dl_ref, lam_init):
    dl = dl_ref[...]
    s1 = jnp.sum(dl[0:1] * dl[1:2], axis=-1, keepdims=True)
    s2 = jnp.sum(dl[2:3] * dl[3:4], axis=-1, keepdims=True)
    return jnp.exp(s1) - jnp.exp(s2) + lam_init


def _subln(a, gain, lam_init):
    y = a * lax.rsqrt(jnp.mean(a * a, axis=-1, keepdims=True) + EPS) * gain
    return y * (1.0 - lam_init)


def _attn_prompt_kernel(*refs, kind, seq, tq, lam_init):
    if kind == "diff":
        c31_ref, q_ref, k_ref, v_ref, bias_ref, dl_ref, gain_ref, o_ref = refs
        c31 = c31_ref[pl.program_id(1)]
        d0 = bias_ref[0, 0]
        d1 = bias_ref[0, 1]
        d0 = jnp.concatenate([d0, d0], axis=0)
        d1 = jnp.concatenate([d1, d1], axis=0)
        lam = _lambda(dl_ref, lam_init)
        lane = lax.broadcasted_iota(jnp.int32, (tq, HEAD_DIM), 1)
    else:
        q_ref, k_ref, v_ref, fk_ref, o_ref = refs
        fk = fk_ref[0, 0]
    n_rows = 2 * tq if kind == "diff" else tq
    q_idx = lax.broadcasted_iota(jnp.int32, (n_rows, tq), 0)
    q_idx = jnp.where(q_idx >= tq, q_idx - tq, q_idx)
    vis = q_idx >= lax.broadcasted_iota(jnp.int32, (n_rows, tq), 1)

    for i in range(seq // tq):
        n = (i + 1) * tq
        q = q_ref[0, i * tq:n, :]
        if kind == "diff":
            zero = jnp.zeros_like(q)
            q = jnp.concatenate([jnp.where(lane < DH_HALF, q, zero),
                                 jnp.where(lane >= DH_HALF, q, zero)], axis=0)
        s = _dot_nt(q, k_ref[0, 0:n, :])
        parts = []
        if kind == "diff":
            if n > 2 * tq:
                parts.append(s[:, :n - 2 * tq] + c31)
            if n > tq:
                parts.append(s[:, n - 2 * tq:n - tq] + d1)
            parts.append(jnp.where(vis, s[:, n - tq:] + d0, NEG_INF))
        else:
            if n > tq:
                parts.append(s[:, :n - tq] - fk[:, :n - tq])
            parts.append(jnp.where(vis, s[:, n - tq:] - fk[:, n - tq:n], NEG_INF))
        mx = functools.reduce(jnp.maximum, [jnp.max(p, axis=-1, keepdims=True) for p in parts])
        parts = [jnp.exp(p - mx) for p in parts]
        den = functools.reduce(jnp.add, [jnp.sum(p, axis=-1, keepdims=True) for p in parts])
        pm = [p.astype(BF16) for p in parts]
        pm = pm[0] if len(pm) == 1 else jnp.concatenate(pm, axis=1)
        o = _dot(pm, v_ref[0, 0:n, :]) / den
        if kind == "diff":
            o = _subln(o[:tq] - lam * o[tq:], gain_ref[...], lam_init)
        o_ref[i * tq:n, :] = o.astype(o_ref.dtype)


def _attn_prompt(kind, q_hm, k_hm, v_hm, batch, seq, *, bias=None, c31=None, dl=None, gain=None,
                 fk=None, lam_init=0.0, tq=256):
    H, M, _ = q_hm.shape
    tq = _tile(seq, tq, LANES)
    hm_spec = pl.BlockSpec((1, seq, HEAD_DIM), lambda b, h: (h, b, 0))
    in_specs = [hm_spec, hm_spec, hm_spec]
    args = [q_hm, k_hm, v_hm]
    if kind == "diff":
        in_specs = [pl.BlockSpec(memory_space=pltpu.SMEM)] + in_specs + [
            pl.BlockSpec((1, 2, tq, tq), lambda b, h: (h, 0, 0, 0)),
            pl.BlockSpec(dl.shape, lambda b, h: (0, 0)),
            pl.BlockSpec((1, HEAD_DIM), lambda b, h: (0, 0)),
        ]
        args = [c31] + args + [bias, dl, gain.reshape(1, HEAD_DIM)]
    else:
        in_specs.append(pl.BlockSpec((1, 1, 1, seq), lambda b, h: (h, b, 0, 0)))
        args.append(fk.reshape(H, batch, 1, seq))
    return pl.pallas_call(
        functools.partial(_attn_prompt_kernel, kind=kind, seq=seq, tq=tq, lam_init=lam_init),
        grid=(batch, H),
        in_specs=in_specs,
        out_specs=pl.BlockSpec((seq, HEAD_DIM), lambda b, h: (b, h)),
        out_shape=jax.ShapeDtypeStruct((M, H * HEAD_DIM), BF16),
        compiler_params=_params(2),
        name="attn_prompt_" + kind,
    )(*args)


def _f_past_kernel(*refs, pps):
    lf_pages = refs[1:1 + pps]
    o_ref, carry_ref = refs[1 + pps], refs[2 + pps]
    page = lf_pages[0].shape[0]

    @pl.when(pl.program_id(1) == 0)
    def _():
        carry_ref[...] = jnp.zeros_like(carry_ref)

    later = (lax.broadcasted_iota(jnp.int32, (page, page), 1)
             > lax.broadcasted_iota(jnp.int32, (page, page), 0))
    later = jnp.where(later, 1.0, 0.0).astype(BF16)
    carry = carry_ref[...]
    for p in reversed(range(pps)):
        lf = lf_pages[p][...]
        o_ref[0, p * page:(p + 1) * page, :] = -(_dot3_rhs(later, lf) + carry)
        carry = carry + jnp.sum(lf, axis=0, keepdims=True)
    carry_ref[...] = carry


def _f_past(layer, page_table, cache_lf):
    B, n_pages = page_table.shape
    _, _, page, H = cache_lf.shape
    pps = _tile(n_pages, PAGES_PER_SUM_STEP, 1)
    n_steps = n_pages // pps

    def page_spec(p):
        return pl.BlockSpec((None, None, page, H),
                            lambda b, c, pt: (layer, pt[b, (n_steps - 1 - c) * pps + p], 0, 0))

    return pl.pallas_call(
        functools.partial(_f_past_kernel, pps=pps),
        grid_spec=pltpu.PrefetchScalarGridSpec(
            num_scalar_prefetch=1,
            grid=(B, n_steps),
            in_specs=[page_spec(p) for p in range(pps)],
            out_specs=pl.BlockSpec((1, pps * page, H), lambda b, c, pt: (b, n_steps - 1 - c, 0)),
            scratch_shapes=[pltpu.VMEM((1, H), F32)],
        ),
        out_shape=jax.ShapeDtypeStruct((B, n_pages * page, H), F32),
        compiler_params=_params(2),
        name="f_past",
    )(page_table, *([cache_lf] * pps))


def _attn_decode_kernel(*refs, kind, pps, n_new, qp, lam_init):
    refs = list(refs)
    refs.pop(0)
    k_pages = [refs.pop(0) for _ in range(pps)]
    v_pages = [refs.pop(0) for _ in range(pps)]
    q_ref, knew_ref, vnew_ref, mpage_ref, mnew_ref = (refs.pop(0) for _ in range(5))
    if kind == "diff":
        blast_ref, bnew_ref, dl_ref, gain_ref = (refs.pop(0) for _ in range(4))
    else:
        fk_ref, fnew_ref = refs.pop(0), refs.pop(0)
    o_ref, kbuf, vbuf, m_ref, l_ref, acc_ref = refs
    c = pl.program_id(1)
    page, n_heads, _ = k_pages[0].shape
    rpp = page * n_heads
    q = q_ref[0]

    def update(parts, v_bf):
        m_old = m_ref[...]
        mx = functools.reduce(jnp.maximum, [jnp.max(p, axis=-1, keepdims=True) for p in parts])
        m_new = jnp.maximum(m_old, mx)
        alpha = jnp.exp(m_old - m_new)
        parts = [jnp.exp(p - m_new) for p in parts]
        den = functools.reduce(jnp.add, [jnp.sum(p, axis=-1, keepdims=True) for p in parts])
        pm = [p.astype(BF16) for p in parts]
        pm = pm[0] if len(pm) == 1 else jnp.concatenate(pm, axis=1)
        l_ref[...] = alpha * l_ref[...] + den
        acc_ref[...] = alpha * acc_ref[...] + _dot(pm, v_bf)
        m_ref[...] = m_new

    @pl.when(c == 0)
    def _():
        m_ref[...] = jnp.full_like(m_ref, NEG_INF)
        l_ref[...] = jnp.zeros_like(l_ref)
        acc_ref[...] = jnp.zeros_like(acc_ref)
        s = _dot_nt(q, knew_ref[0].astype(BF16)) + mnew_ref[...]
        if kind == "diff":
            s = s + bnew_ref[...]
        else:
            s = s - fnew_ref[0]
        update([s], vnew_ref[0].astype(BF16))

    for p in range(pps):
        kbuf[p * rpp:(p + 1) * rpp, :] = k_pages[p][...].reshape(rpp, HEAD_DIM).astype(BF16)
        vbuf[p * rpp:(p + 1) * rpp, :] = v_pages[p][...].reshape(rpp, HEAD_DIM).astype(BF16)
    s = _dot_nt(q, kbuf[...])
    parts = []
    for p in range(pps):
        sp = s[:, p * rpp:(p + 1) * rpp] + mpage_ref[...]
        if kind == "fox":
            sp = sp - fk_ref[0, 0, :, p * rpp:(p + 1) * rpp]
        elif p == pps - 1:
            sp = sp + jnp.where(c == 0, blast_ref[...], 0.0)
        parts.append(sp)
    update(parts, vbuf[...])

    @pl.when(c == pl.num_programs(1) - 1)
    def _():
        o = acc_ref[...] / l_ref[...]
        if kind == "diff":
            half = n_heads * qp
            o = _subln(o[:half] - _lambda(dl_ref, lam_init) * o[half:], gain_ref[...], lam_init)
        for h in range(n_heads):
            o_ref[0, :, h * HEAD_DIM:(h + 1) * HEAD_DIM] = o[h * qp:h * qp + n_new, :]


def _attn_decode(kind, layer, page_table, cache_k, cache_v, q, k_new, v_new, *, f_past=None,
                 f_new=None, bias_last=None, bias_new=None, dl=None, gain=None, lam_init=0.0):
    B, T, H, _ = q.shape
    W = H * HEAD_DIM
    page = cache_k.shape[2]
    n_pages = page_table.shape[1]
    pps = _tile(n_pages, PAGES_PER_STEP, 1)
    n_chunks = n_pages // pps
    rpp = page * H
    qp = T
    assert SUBLANES % T == 0 and page >= MAX_DISTANCE and SUBLANES * H == LANES
    n_maps = 2 if kind == "diff" else 1
    scale = (DH_HALF if kind == "diff" else HEAD_DIM) ** -0.5
    qh = jnp.transpose(q * scale, (0, 2, 1, 3)).reshape(B, H * qp, HEAD_DIM)
    if kind == "diff":
        lo = jnp.arange(HEAD_DIM) < DH_HALF
        qh = jnp.concatenate([jnp.where(lo, qh, 0.0), jnp.where(lo, 0.0, qh)], axis=1)
    qh = qh.astype(BF16)
    nq = n_maps * H * qp
    new_rows = lambda x: jnp.pad(x, ((0, 0), (0, SUBLANES - T), (0, 0), (0, 0))).reshape(B, LANES, HEAD_DIM)

    row_head = (np.arange(nq) // qp) % H
    row_q = np.arange(nq) % qp
    same_head = row_head[:, None] == (np.arange(rpp) % H)[None, :]
    mask_page = np.where(same_head, 0.0, NEG_INF).astype(np.float32)
    new_key = np.arange(LANES) // H
    ok_new = np.logical_and(same_head[:, :LANES], new_key[None, :] <= row_q[:, None])
    mask_new = np.where(ok_new, 0.0, NEG_INF).astype(np.float32)

    def page_spec(p):
        return pl.BlockSpec((None, None, page, H, HEAD_DIM),
                            lambda b, c, pt: (layer, pt[b, (n_chunks - 1 - c) * pps + p], 0, 0, 0))

    per_seq = lambda r, cc: pl.BlockSpec((1, r, cc), lambda b, c, pt: (b, 0, 0))
    const2 = lambda shape: pl.BlockSpec(shape, lambda b, c, pt: (0, 0))
    in_specs = ([page_spec(p) for p in range(pps)] + [page_spec(p) for p in range(pps)]
                + [per_seq(nq, HEAD_DIM), per_seq(LANES, HEAD_DIM), per_seq(LANES, HEAD_DIM),
                   const2((nq, rpp)), const2((nq, LANES))])
    args = ([cache_k] * pps + [cache_v] * pps
            + [qh, new_rows(k_new), new_rows(v_new), jnp.asarray(mask_page), jnp.asarray(mask_new)])
    if kind == "diff":
        in_specs += [const2((nq, rpp)), const2((nq, LANES)), const2(dl.shape), const2((1, HEAD_DIM))]
        args += [bias_last, bias_new, dl, gain.reshape(1, HEAD_DIM)]
    else:
        in_specs += [pl.BlockSpec((1, 1, 1, pps * rpp), lambda b, c, pt: (b, n_chunks - 1 - c, 0, 0)),
                     per_seq(1, LANES)]
        f_new_rows = jnp.pad(f_new, ((0, 0), (0, SUBLANES - T), (0, 0))).reshape(B, 1, LANES)
        args += [f_past.reshape(B, n_chunks, 1, pps * rpp), f_new_rows]

    return pl.pallas_call(
        functools.partial(_attn_decode_kernel, kind=kind, pps=pps, n_new=T, qp=qp, lam_init=lam_init),
        grid_spec=pltpu.PrefetchScalarGridSpec(
            num_scalar_prefetch=1,
            grid=(B, n_chunks),
            in_specs=in_specs,
            out_specs=pl.BlockSpec((1, T, W), lambda b, c, pt: (b, 0, 0)),
            scratch_shapes=[pltpu.VMEM((pps * rpp, HEAD_DIM), BF16), pltpu.VMEM((pps * rpp, HEAD_DIM), BF16),
                            pltpu.VMEM((nq, 1), F32), pltpu.VMEM((nq, 1), F32),
                            pltpu.VMEM((nq, HEAD_DIM), F32)],
        ),
        out_shape=jax.ShapeDtypeStruct((B, T, W), F32),
        compiler_params=_params(2),
        name="attn_decode_" + kind,
    )(page_table, *args)


def _decode_bias(rel_bias, T, page):
    H = rel_bias.shape[1]
    far = rel_bias[N_BUCKETS - 1][:, None, None]
    last = _rel_bias_tiles(rel_bias, T, page, (page,))[:, 0] - far
    new = _rel_bias_tiles(rel_bias, T, SUBLANES, (0,))[:, 0] - far
    expand = lambda t: jnp.tile(jnp.repeat(t, H, axis=-1).reshape(H * T, -1), (2, 1))
    return expand(last), expand(new)


def _trunk(x, mod, past, wts, bias_prompt, bias_decode):
    (ada_table, norm_attn, norm_mlp, norm_final, w_in, b_forget, diff_lambda, subln_gain,
     rel_bias, w_out, w_up, w_down) = wts
    B, T, D = x.shape
    depth = w_in.shape[0]
    H = w_in.shape[2] // (6 * HEAD_DIM + 1)
    W = H * HEAD_DIM
    M = B * T
    prompt = past is None
    groups = B if prompt else 1
    off_f = 6 * W
    c31 = rel_bias[N_BUCKETS - 1]

    def per_group(v):
        return v[:, None, :] if prompt else jnp.repeat(v, T, axis=0)[None]

    x = x.reshape(M, D)
    rows = ([], [], [], [], [])
    for l in range(depth):
        m = mod + ada_table[l][None]
        shift_a, scale_a, gate_a, shift_m, scale_m, gate_m = [per_group(m[:, i]) for i in range(N_MOD)]
        lam_init = 0.8 - 0.6 * math.exp(-0.3 * l)
        h = _norm(x, norm_attn[l], scale_a, shift_a, groups=groups)
        proj = functools.partial(_matmul, h, w_in, l, groups=groups)
        if prompt:
            qa = proj(0 * W, W, want_hm=True, scale=DH_HALF ** -0.5)
            ka, ka_hm = proj(1 * W, W, main_dtype=F32, want_hm=True)
            va, va_hm = proj(2 * W, W, main_dtype=F32, want_hm=True)
            qf = proj(3 * W, W, want_hm=True, scale=HEAD_DIM ** -0.5)
            kf, kf_hm = proj(4 * W, W, main_dtype=F32, want_hm=True)
            vf, vf_hm = proj(5 * W, W, main_dtype=F32, want_hm=True)
        else:
            qa, ka, va, qf, kf, vf = [proj(i * W, W, main_dtype=F32) for i in range(6)]
        logf, cum_t = _forget(h, w_in[l][:, off_f:], b_forget[l], T)
        if prompt:
            oa = _attn_prompt("diff", qa, ka_hm, va_hm, B, T, bias=bias_prompt, c31=c31,
                              dl=diff_lambda[l], gain=subln_gain[l], lam_init=lam_init)
            of = _attn_prompt("fox", qf, kf_hm, vf_hm, B, T, fk=cum_t)
        else:
            page_table, c_ka, c_va, c_kf, c_vf, c_lf = past
            r4 = lambda a: a.reshape(B, T, H, HEAD_DIM)
            blast, bnew = bias_decode
            oa = _attn_decode("diff", l, page_table, c_ka, c_va, r4(qa), r4(ka), r4(va),
                              bias_last=blast, bias_new=bnew, dl=diff_lambda[l],
                              gain=subln_gain[l], lam_init=lam_init)
            of = _attn_decode("fox", l, page_table, c_kf, c_vf, r4(qf), r4(kf), r4(vf),
                              f_past=_f_past(l, page_table, c_lf), f_new=cum_t.T.reshape(B, T, H))
            oa = oa.reshape(M, W).astype(BF16)
            of = of.reshape(M, W).astype(BF16)
        o = jnp.concatenate([oa, of], axis=-1)
        x = _matmul(o, w_out, l, 0, D, groups=groups, epi="resid", gate=gate_a, resid=x, main_dtype=F32)
        h2 = _norm(x, norm_mlp[l], scale_m, shift_m, groups=groups)
        u = _matmul(h2, w_up, l, 0, w_up.shape[2], groups=groups, epi="relu2", main_dtype=BF16)
        x = _matmul(u, w_down, l, 0, D, groups=groups, tk=2048, epi="resid", gate=gate_m, resid=x,
                    main_dtype=F32)
        for lst, r in zip(rows, (ka, va, kf, vf, logf)):
            lst.append(r)
    y = _norm(x, norm_final, out_dtype=F32).reshape(B, T, D)
    kv = [jnp.stack(lst, axis=0).reshape(depth, B, T, H, HEAD_DIM) for lst in rows[:4]]
    return y, kv + [jnp.stack(rows[4], axis=0).reshape(depth, B, T, H)]


def kernel(x_prompt, x_sample, cache_k_diff, cache_v_diff, cache_k_fox, cache_v_fox, cache_logf_fox,
           page_table, c_prompt, c_sample, w_ada, b_ada, ada_table, norm_attn, norm_mlp, norm_final,
           w_in, b_forget, diff_lambda, subln_gain, rel_bias, w_out, w_up, w_down):
    wts = (ada_table, norm_attn, norm_mlp, norm_final, w_in, b_forget, diff_lambda, subln_gain,
           rel_bias, w_out, w_up, w_down)
    bp, _, D = x_prompt.shape
    bs, ts, _ = x_sample.shape

    c_all = jnp.concatenate([c_prompt, c_sample], axis=0)
    rows = -(-c_all.shape[0] // SUBLANES) * SUBLANES
    c_all = jnp.pad(c_all, ((0, rows - c_all.shape[0]), (0, 0)))
    mod = _matmul(c_all, w_ada[None], 0, 0, w_ada.shape[1], tm=rows, silu=True, epi="bias",
                  bias=b_ada.reshape(1, -1), main_dtype=F32)
    mod = mod.reshape(rows, N_MOD, D)

    tq = _tile(x_prompt.shape[1], 256, LANES)
    bias_prompt = _rel_bias_tiles(rel_bias, tq, tq, (0, tq))
    y_prompt, rp = _trunk(x_prompt, mod[:bp], None, wts, bias_prompt, None)

    past = (page_table, cache_k_diff, cache_v_diff, cache_k_fox, cache_v_fox, cache_logf_fox)
    bias_decode = _decode_bias(rel_bias, ts, cache_k_diff.shape[2])
    y_sample, rs = _trunk(x_sample, mod[bp:bp + bs], past, wts, None, bias_decode)
    return (y_prompt, y_sample, rp[0], rp[1], rp[2], rp[3], rp[4], rs[0], rs[1], rs[2], rs[3], rs[4])
```

```python
import functools
import math

import numpy as np
import jax
import jax.numpy as jnp
from jax import lax
from jax.experimental import pallas as pl
from jax.experimental.pallas import tpu as pltpu

HEAD_DIM = 128
DH_HALF = HEAD_DIM // 2
N_BUCKETS = 32
MAX_DISTANCE = 128
EPS = 1e-6
N_MOD = 6
NEG_INF = -1e30
LANES = 128
SUBLANES = 8
VMEM_LIMIT = 56 * 1024 * 1024
PAGES_PER_STEP = 4
PAGES_PER_SUM_STEP = 16
F32 = jnp.float32
BF16 = jnp.bfloat16

_NT = (((1,), (1,)), ((), ()))


def _params(n_axes):
    return pltpu.CompilerParams(dimension_semantics=("arbitrary",) * n_axes,
                                vmem_limit_bytes=VMEM_LIMIT)


def _tile(dim, pref, align):
    t = (min(pref, dim) // align) * align
    while t >= align:
        if dim % t == 0:
            return t
        t -= align
    return dim


def _split3(x):
    hi = x.astype(BF16)
    r1 = x - hi.astype(F32)
    mid = r1.astype(BF16)
    lo = (r1 - mid.astype(F32)).astype(BF16)
    return hi, mid, lo


def _dot(a, b):
    return jnp.dot(a, b, preferred_element_type=F32)


def _dot_nt(a, b):
    return lax.dot_general(a, b, _NT, preferred_element_type=F32)


def _dot3_rhs(mat, x):
    hi, mid, lo = _split3(x)
    return _dot(mat, hi) + _dot(mat, mid) + _dot(mat, lo)


def _dot3_lhs(x, mat):
    hi, mid, lo = _split3(x)
    return _dot(hi, mat) + _dot(mid, mat) + _dot(lo, mat)


def _log_sigmoid(z):
    return jnp.minimum(z, 0.0) - jnp.log1p(jnp.exp(-jnp.abs(z)))


def _mm_kernel(*refs, nk, tm, silu, scale, epi, main, want_hm, aliased, w_rows):
    a_ref, w_ref = refs[0], refs[1]
    pos = 2
    bias_ref = gate_ref = resid_ref = main_ref = hm_ref = None
    if epi == "bias":
        bias_ref = refs[pos]
        pos += 1
    if epi == "resid":
        gate_ref, resid_ref = refs[pos], refs[pos + 1]
        pos += 2
    if aliased:
        pos += 1
    if main is not None:
        main_ref = refs[pos]
        pos += 1
    if want_hm:
        hm_ref = refs[pos]
        pos += 1
    wbf_ref = refs[pos]
    k = pl.program_id(2)
    m = pl.program_id(3)

    @pl.when(m == 0)
    def _():
        if w_rows == "in":
            wbf_ref[...] = w_ref[...].astype(BF16)
        else:
            tk = wbf_ref.shape[0]
            ck = _tile(tk, 512, LANES)
            for c in range(tk // ck):
                wbf_ref[c * ck:(c + 1) * ck, :] = w_ref[:, c * ck:(c + 1) * ck].T.astype(BF16)

    a = a_ref[...]
    if silu:
        a = (a * jax.nn.sigmoid(a)).astype(BF16)
    acc = _dot(a, wbf_ref[...])

    if nk == 1:
        if epi == "bias":
            acc = acc + bias_ref[...]
        elif epi == "relu2":
            r = jnp.maximum(acc, 0.0)
            acc = r * r
        elif epi == "resid":
            acc = resid_ref[...] + gate_ref[0] * acc
        if main == "flat":
            main_ref[...] = acc.astype(main_ref.dtype)
        elif main == "heads":
            for hh in range(main_ref.shape[1]):
                main_ref[:, hh, :] = acc[:, hh * HEAD_DIM:(hh + 1) * HEAD_DIM]
        if want_hm:
            for hh in range(hm_ref.shape[0]):
                piece = acc[:, hh * HEAD_DIM:(hh + 1) * HEAD_DIM]
                hm_ref[hh] = (piece * scale).astype(hm_ref.dtype)
    else:
        rows = pl.ds(pl.multiple_of(m * tm, tm), tm)

        @pl.when(k == 0)
        def _():
            main_ref[rows, :] = acc

        @pl.when(jnp.logical_and(k > 0, k < nk - 1))
        def _():
            main_ref[rows, :] += acc

        @pl.when(k == nk - 1)
        def _():
            main_ref[rows, :] = resid_ref[...] + gate_ref[0] * (main_ref[rows, :] + acc)


def _matmul(a, w, layer, n0, n, *, groups=1, tm=256, tn=1024, tk=None, silu=False, scale=1.0,
            epi="none", bias=None, gate=None, resid=None, main_dtype=None, stack=None,
            want_hm=False, w_rows="in"):
    M, K = a.shape
    mg = M // groups
    tm = _tile(mg, tm, SUBLANES)
    tn = _tile(n, tn, LANES)
    tk = K if tk is None else _tile(K, tk, LANES)
    assert n0 % tn == 0 and M % groups == 0
    nm, nj, nk = mg // tm, n // tn, K // tk
    jb0 = n0 // tn
    hpt = tn // HEAD_DIM
    main = "heads" if stack is not None else ("flat" if main_dtype is not None else None)
    assert nk == 1 or (epi == "resid" and main == "flat" and not want_hm)
    assert groups == 1 or nk > 1

    def row(g, m):
        return g * nm + m

    if w_rows == "in":
        w_spec = pl.BlockSpec((None, tk, tn), lambda g, j, k, m: (layer, k, jb0 + j))
    else:
        w_spec = pl.BlockSpec((None, tn, tk), lambda g, j, k, m: (layer, jb0 + j, k))
    in_specs = [pl.BlockSpec((tm, tk), lambda g, j, k, m: (row(g, m), k)), w_spec]
    args = [a, w]
    if epi == "bias":
        in_specs.append(pl.BlockSpec((1, tn), lambda g, j, k, m: (0, j)))
        args.append(bias)
    if epi == "resid":
        n_seq, rb, _ = gate.shape
        rows_per_seq = M // n_seq
        assert (rb == 1 and rows_per_seq % tm == 0) or (rb == tm and n_seq == 1 and M == tm)
        in_specs.append(pl.BlockSpec(
            (1, rb, tn), lambda g, j, k, m: ((row(g, m) * tm) // rows_per_seq, 0, j)))
        if nk == 1:
            in_specs.append(pl.BlockSpec((tm, tn), lambda g, j, k, m: (row(g, m), j)))
        else:
            in_specs.append(pl.BlockSpec(
                (tm, tn), lambda g, j, k, m: (g * nm + jnp.where(k == nk - 1, m, 0), j)))
        args += [gate, resid]

    out_shape, out_specs, aliases = [], [], {}
    if main == "heads":
        depth, buf = stack
        assert hpt % SUBLANES == 0 and nk == 1
        if buf is not None:
            aliases = {len(args): 0}
            in_specs.append(pl.BlockSpec(memory_space=pl.ANY))
            args.append(buf)
        out_shape.append(jax.ShapeDtypeStruct((depth, M, n // HEAD_DIM, HEAD_DIM), F32))
        out_specs.append(pl.BlockSpec((None, tm, hpt, HEAD_DIM),
                                      lambda g, j, k, m: (layer, row(g, m), j, 0)))
    elif main == "flat":
        out_shape.append(jax.ShapeDtypeStruct((M, n), main_dtype))
        if nk == 1:
            out_specs.append(pl.BlockSpec((tm, tn), lambda g, j, k, m: (row(g, m), j)))
        else:
            out_specs.append(pl.BlockSpec((mg, tn), lambda g, j, k, m: (g, j)))
    if want_hm:
        out_shape.append(jax.ShapeDtypeStruct((n // HEAD_DIM, M, HEAD_DIM), BF16))
        out_specs.append(pl.BlockSpec((hpt, tm, HEAD_DIM), lambda g, j, k, m: (j, row(g, m), 0)))

    kern = functools.partial(_mm_kernel, nk=nk, tm=tm, silu=silu, scale=scale, epi=epi,
                             main=main, want_hm=want_hm, aliased=bool(aliases), w_rows=w_rows)
    outs = pl.pallas_call(
        kern,
        grid=(groups, nj, nk, nm),
        in_specs=in_specs,
        out_specs=out_specs,
        out_shape=out_shape,
        scratch_shapes=[pltpu.VMEM((tk, tn), BF16)],
        input_output_aliases=aliases,
        compiler_params=_params(4),
        name="matmul_" + epi,
    )(*args)
    return outs[0] if len(outs) == 1 else tuple(outs)


def _norm_kernel(*refs, modulate):
    if modulate:
        x_ref, g_ref, sc_ref, sh_ref, o_ref = refs
    else:
        x_ref, g_ref, o_ref = refs
    x = x_ref[...]
    y = x * lax.rsqrt(jnp.mean(x * x, axis=-1, keepdims=True) + EPS) * g_ref[...]
    if modulate:
        y = y * (1.0 + sc_ref[0]) + sh_ref[0]
    o_ref[...] = y.astype(o_ref.dtype)


def _norm(x, g, scale=None, shift=None, *, groups=1, out_dtype=BF16, tm=256):
    M, D = x.shape
    mg = M // groups
    tm = _tile(mg, tm, SUBLANES)
    nm = mg // tm
    modulate = scale is not None
    in_specs = [pl.BlockSpec((tm, D), lambda i: (i, 0)), pl.BlockSpec((1, D), lambda i: (0, 0))]
    args = [x, g.reshape(1, D)]
    if modulate:
        rb = scale.shape[1]
        assert rb in (1, tm)
        spec = pl.BlockSpec((1, rb, D), lambda i: (i // nm, 0, 0))
        in_specs += [spec, spec]
        args += [scale, shift]
    return pl.pallas_call(
        functools.partial(_norm_kernel, modulate=modulate),
        grid=(M // tm,),
        in_specs=in_specs,
        out_specs=pl.BlockSpec((tm, D), lambda i: (i, 0)),
        out_shape=jax.ShapeDtypeStruct((M, D), out_dtype),
        compiler_params=_params(1),
        name="rmsnorm",
    )(*args)


def _forget_kernel(h_ref, wf_ref, b_ref, logf_ref, cumt_ref, carry_ref, *, tm, seq):
    i = pl.program_id(0)
    n_heads = logf_ref.shape[1]
    lf = _log_sigmoid(_dot_nt(h_ref[...], wf_ref[...].astype(BF16)) + b_ref[...])
    logf_ref[...] = lf[:, :n_heads]
    lt = lf.T[:n_heads]

    src = lax.broadcasted_iota(jnp.int32, (tm, tm), 0)
    dst = lax.broadcasted_iota(jnp.int32, (tm, tm), 1)
    tri = src <= dst
    if tm > seq:
        tri = jnp.logical_and(tri, src // seq == dst // seq)
    tri = jnp.where(tri, 1.0, 0.0).astype(BF16)
    cs = _dot3_lhs(lt, tri)
    if tm < seq:
        @pl.when((i * tm) % seq == 0)
        def _():
            carry_ref[...] = jnp.zeros_like(carry_ref)

        cs = cs + carry_ref[...]
        carry_ref[...] = cs[:, tm - 1:tm]
    cumt_ref[...] = cs


def _forget(h, wf, layer, b, seq):
    M, D = h.shape
    H = b.shape[0]
    rows = max(M, LANES)
    if rows != M:
        h = jnp.pad(h, ((0, rows - M), (0, 0)))
    tm = _tile(rows, 256, LANES)
    assert seq % tm == 0 or tm % seq == 0
    logf, cum_t = pl.pallas_call(
        functools.partial(_forget_kernel, tm=tm, seq=seq),
        grid=(rows // tm,),
        in_specs=[
            pl.BlockSpec((tm, D), lambda i: (i, 0)),
            pl.BlockSpec((None, LANES, D), lambda i: (layer, 0, 0)),
            pl.BlockSpec((1, LANES), lambda i: (0, 0)),
        ],
        out_specs=[pl.BlockSpec((tm, H), lambda i: (i, 0)), pl.BlockSpec((H, tm), lambda i: (0, i))],
        out_shape=[jax.ShapeDtypeStruct((rows, H), F32), jax.ShapeDtypeStruct((H, rows), F32)],
        scratch_shapes=[pltpu.VMEM((H, 1), F32)],
        compiler_params=_params(1),
        name="forget_gate",
    )(h, wf, jnp.pad(b, (0, LANES - H)).reshape(1, LANES))
    return logf[:M], cum_t[:, :M]


def _bucket_thresholds():
    n = np.arange(0, 4 * MAX_DISTANCE, dtype=np.int32)
    max_exact = N_BUCKETS // 2
    nf = np.maximum(n, 1).astype(np.float32)
    large = max_exact + (np.log(nf / np.float32(max_exact)) / np.float32(math.log(MAX_DISTANCE / max_exact))
                         * np.float32(N_BUCKETS - max_exact)).astype(np.int32)
    bucket = np.where(n < max_exact, n, np.minimum(large, N_BUCKETS - 1))
    assert np.all(np.diff(bucket) >= 0) and bucket[-1] == N_BUCKETS - 1
    return [int(np.argmax(bucket >= t)) for t in range(N_BUCKETS)]


def _rel_bias_kernel(rb_ref, o_ref, *, rel0s, thr):
    h = pl.program_id(0)
    _, _, R, C = o_ref.shape
    d = lax.broadcasted_iota(jnp.int32, (R, C), 0) - lax.broadcasted_iota(jnp.int32, (R, C), 1)
    for p, rel0 in enumerate(rel0s):
        rel = d + rel0
        acc = jnp.full((R, C), rb_ref[0, h], F32)
        for t in range(1, N_BUCKETS):
            acc = jnp.where(rel >= thr[t], rb_ref[t, h], acc)
        o_ref[0, p] = acc


def _rel_bias_tiles(rel_bias, R, C, rel0s):
    H = rel_bias.shape[1]
    return pl.pallas_call(
        functools.partial(_rel_bias_kernel, rel0s=tuple(rel0s), thr=_bucket_thresholds()),
        grid=(H,),
        in_specs=[pl.BlockSpec(memory_space=pltpu.SMEM)],
        out_specs=pl.BlockSpec((1, len(rel0s), R, C), lambda h: (h, 0, 0, 0)),
        out_shape=jax.ShapeDtypeStruct((H, len(rel0s), R, C), F32),
        compiler_params=_params(1),
        name="rel_bias_tiles",
    )(rel_bias)


def _lambda(dl_ref, lam_init):
    dl = dl_ref[...]
    s1 = jnp.sum(dl[0:1] * dl[1:2], axis=-1, keepdims=True)
    s2 = jnp.sum(dl[2:3] * dl[3:4], axis=-1, keepdims=True)
    return jnp.exp(s1) - jnp.exp(s2) + lam_init


def _subln(a, gain, lam_init):
    y = a * lax.rsqrt(jnp.mean(a * a, axis=-1, keepdims=True) + EPS) * gain
    return y * (1.0 - lam_init)


def _attn_prompt_kernel(*refs, kind, seq, tq, lam_init):
    if kind == "diff":
        c31_ref, q_ref, k_ref, v_ref, bias_ref, dl_ref, gain_ref, o_ref = refs
        c31 = c31_ref[pl.program_id(1)]
        d0 = bias_ref[0, 0]
        d1 = bias_ref[0, 1]
        d0 = jnp.concatenate([d0, d0], axis=0)
        d1 = jnp.concatenate([d1, d1], axis=0)
        lam = _lambda(dl_ref, lam_init)
        lane = lax.broadcasted_iota(jnp.int32, (tq, HEAD_DIM), 1)
    else:
        q_ref, k_ref, v_ref, fk_ref, _, o_ref = refs
        fk = fk_ref[0, 0]
    n_rows = 2 * tq if kind == "diff" else tq
    q_idx = lax.broadcasted_iota(jnp.int32, (n_rows, tq), 0)
    q_idx = jnp.where(q_idx >= tq, q_idx - tq, q_idx)
    vis = q_idx >= lax.broadcasted_iota(jnp.int32, (n_rows, tq), 1)

    for i in range(seq // tq):
        n = (i + 1) * tq
        q = q_ref[0, i * tq:n, :]
        if kind == "diff":
            zero = jnp.zeros_like(q)
            q = jnp.concatenate([jnp.where(lane < DH_HALF, q, zero),
                                 jnp.where(lane >= DH_HALF, q, zero)], axis=0)
        s = _dot_nt(q, k_ref[0, 0:n, :])
        parts = []
        if kind == "diff":
            if n > 2 * tq:
                parts.append(s[:, :n - 2 * tq] + c31)
            if n > tq:
                parts.append(s[:, n - 2 * tq:n - tq] + d1)
            parts.append(jnp.where(vis, s[:, n - tq:] + d0, NEG_INF))
        else:
            if n > tq:
                parts.append(s[:, :n - tq] - fk[:, :n - tq])
            parts.append(jnp.where(vis, s[:, n - tq:] - fk[:, n - tq:n], NEG_INF))
        mx = functools.reduce(jnp.maximum, [jnp.max(p, axis=-1, keepdims=True) for p in parts])
        parts = [jnp.exp(p - mx) for p in parts]
        den = functools.reduce(jnp.add, [jnp.sum(p, axis=-1, keepdims=True) for p in parts])
        pm = [p.astype(BF16) for p in parts]
        pm = pm[0] if len(pm) == 1 else jnp.concatenate(pm, axis=1)
        o = _dot(pm, v_ref[0, 0:n, :]) / den
        if kind == "diff":
            o = _subln(o[:tq] - lam * o[tq:], gain_ref[...], lam_init)
        o_ref[i * tq:n, :] = o.astype(o_ref.dtype)


def _attn_prompt(kind, q_hm, k_hm, v_hm, batch, seq, *, bias=None, c31=None, dl=None, gain=None,
                 fk=None, out=None, lam_init=0.0, tq=256):
    H, M, _ = q_hm.shape
    tq = _tile(seq, tq, LANES)
    hm_spec = pl.BlockSpec((1, seq, HEAD_DIM), lambda b, h: (h, b, 0))
    in_specs = [hm_spec, hm_spec, hm_spec]
    args = [q_hm, k_hm, v_hm]
    aliases = {}
    if kind == "diff":
        in_specs = [pl.BlockSpec(memory_space=pltpu.SMEM)] + in_specs + [
            pl.BlockSpec((1, 2, tq, tq), lambda b, h: (h, 0, 0, 0)),
            pl.BlockSpec(dl.shape, lambda b, h: (0, 0)),
            pl.BlockSpec((1, HEAD_DIM), lambda b, h: (0, 0)),
        ]
        args = [c31] + args + [bias, dl, gain.reshape(1, HEAD_DIM)]
        head0 = 0
    else:
        in_specs += [pl.BlockSpec((1, 1, 1, seq), lambda b, h: (h, b, 0, 0)),
                     pl.BlockSpec(memory_space=pl.ANY)]
        args += [fk.reshape(H, batch, 1, seq), out]
        aliases = {4: 0}
        head0 = H
    return pl.pallas_call(
        functools.partial(_attn_prompt_kernel, kind=kind, seq=seq, tq=tq, lam_init=lam_init),
        grid=(batch, H),
        in_specs=in_specs,
        out_specs=pl.BlockSpec((seq, HEAD_DIM), lambda b, h: (b, head0 + h)),
        out_shape=jax.ShapeDtypeStruct((M, 2 * H * HEAD_DIM), BF16),
        input_output_aliases=aliases,
        compiler_params=_params(2),
        name="attn_prompt_" + kind,
    )(*args)


def _f_past_kernel(*refs, pps):
    lf_pages = refs[1:1 + pps]
    o_ref, carry_ref = refs[1 + pps], refs[2 + pps]
    page = lf_pages[0].shape[0]

    @pl.when(pl.program_id(1) == 0)
    def _():
        carry_ref[...] = jnp.zeros_like(carry_ref)

    later = (lax.broadcasted_iota(jnp.int32, (page, page), 1)
             > lax.broadcasted_iota(jnp.int32, (page, page), 0))
    later = jnp.where(later, 1.0, 0.0).astype(BF16)
    carry = carry_ref[...]
    for p in reversed(range(pps)):
        lf = lf_pages[p][...]
        o_ref[0, p * page:(p + 1) * page, :] = -(_dot3_rhs(later, lf) + carry)
        carry = carry + jnp.sum(lf, axis=0, keepdims=True)
    carry_ref[...] = carry


def _f_past(layer, page_table, cache_lf):
    B, n_pages = page_table.shape
    _, _, page, H = cache_lf.shape
    pps = _tile(n_pages, PAGES_PER_SUM_STEP, 1)
    n_steps = n_pages // pps

    def page_spec(p):
        return pl.BlockSpec((None, None, page, H),
                            lambda b, c, pt: (layer, pt[b, (n_steps - 1 - c) * pps + p], 0, 0))

    return pl.pallas_call(
        functools.partial(_f_past_kernel, pps=pps),
        grid_spec=pltpu.PrefetchScalarGridSpec(
            num_scalar_prefetch=1,
            grid=(B, n_steps),
            in_specs=[page_spec(p) for p in range(pps)],
            out_specs=pl.BlockSpec((1, pps * page, H), lambda b, c, pt: (b, n_steps - 1 - c, 0)),
            scratch_shapes=[pltpu.VMEM((1, H), F32)],
        ),
        out_shape=jax.ShapeDtypeStruct((B, n_pages * page, H), F32),
        compiler_params=_params(2),
        name="f_past",
    )(page_table, *([cache_lf] * pps))


def _attn_decode_kernel(*refs, kind, pps, n_new, qp, lam_init):
    refs = list(refs)
    refs.pop(0)
    k_pages = [refs.pop(0) for _ in range(pps)]
    v_pages = [refs.pop(0) for _ in range(pps)]
    q_ref, knew_ref, vnew_ref, mpage_ref, mnew_ref = (refs.pop(0) for _ in range(5))
    if kind == "diff":
        blast_ref, bnew_ref, dl_ref, gain_ref = (refs.pop(0) for _ in range(4))
    else:
        fk_ref, fnew_ref = refs.pop(0), refs.pop(0)
    o_ref, kbuf, vbuf, m_ref, l_ref, acc_ref = refs
    c = pl.program_id(1)
    page, n_heads, _ = k_pages[0].shape
    rpp = page * n_heads
    q = q_ref[0]

    def update(parts, v_bf):
        m_old = m_ref[...]
        mx = functools.reduce(jnp.maximum, [jnp.max(p, axis=-1, keepdims=True) for p in parts])
        m_new = jnp.maximum(m_old, mx)
        alpha = jnp.exp(m_old - m_new)
        parts = [jnp.exp(p - m_new) for p in parts]
        den = functools.reduce(jnp.add, [jnp.sum(p, axis=-1, keepdims=True) for p in parts])
        pm = [p.astype(BF16) for p in parts]
        pm = pm[0] if len(pm) == 1 else jnp.concatenate(pm, axis=1)
        l_ref[...] = alpha * l_ref[...] + den
        acc_ref[...] = alpha * acc_ref[...] + _dot(pm, v_bf)
        m_ref[...] = m_new

    @pl.when(c == 0)
    def _():
        m_ref[...] = jnp.full_like(m_ref, NEG_INF)
        l_ref[...] = jnp.zeros_like(l_ref)
        acc_ref[...] = jnp.zeros_like(acc_ref)
        s = _dot_nt(q, knew_ref[0].astype(BF16)) + mnew_ref[...]
        if kind == "diff":
            s = s + bnew_ref[...]
        else:
            s = s - fnew_ref[0]
        update([s], vnew_ref[0].astype(BF16))

    for p in range(pps):
        kbuf[p * rpp:(p + 1) * rpp, :] = k_pages[p][...].reshape(rpp, HEAD_DIM).astype(BF16)
        vbuf[p * rpp:(p + 1) * rpp, :] = v_pages[p][...].reshape(rpp, HEAD_DIM).astype(BF16)
    s = _dot_nt(q, kbuf[...])
    parts = []
    for p in range(pps):
        sp = s[:, p * rpp:(p + 1) * rpp] + mpage_ref[...]
        if kind == "fox":
            sp = sp - fk_ref[0, 0, :, p * rpp:(p + 1) * rpp]
        elif p == pps - 1:
            sp = sp + jnp.where(c == 0, blast_ref[...], 0.0)
        parts.append(sp)
    update(parts, vbuf[...])

    @pl.when(c == pl.num_programs(1) - 1)
    def _():
        o = acc_ref[...] / l_ref[...]
        if kind == "diff":
            half = n_heads * qp
            o = _subln(o[:half] - _lambda(dl_ref, lam_init) * o[half:], gain_ref[...], lam_init)
        for h in range(n_heads):
            o_ref[0, :, h * HEAD_DIM:(h + 1) * HEAD_DIM] = o[h * qp:h * qp + n_new, :]


def _attn_decode(kind, layer, page_table, cache_k, cache_v, q, k_new, v_new, *, f_past=None,
                 f_new=None, bias_last=None, bias_new=None, dl=None, gain=None, lam_init=0.0):
    B, T, H, _ = q.shape
    W = H * HEAD_DIM
    page = cache_k.shape[2]
    n_pages = page_table.shape[1]
    pps = _tile(n_pages, PAGES_PER_STEP, 1)
    n_chunks = n_pages // pps
    rpp = page * H
    qp = T
    assert SUBLANES % T == 0 and page >= MAX_DISTANCE and SUBLANES * H == LANES
    n_maps = 2 if kind == "diff" else 1
    scale = (DH_HALF if kind == "diff" else HEAD_DIM) ** -0.5
    qh = jnp.transpose(q * scale, (0, 2, 1, 3)).reshape(B, H * qp, HEAD_DIM)
    if kind == "diff":
        lo = jnp.arange(HEAD_DIM) < DH_HALF
        qh = jnp.concatenate([jnp.where(lo, qh, 0.0), jnp.where(lo, 0.0, qh)], axis=1)
    qh = qh.astype(BF16)
    nq = n_maps * H * qp
    new_rows = lambda x: jnp.pad(x, ((0, 0), (0, SUBLANES - T), (0, 0), (0, 0))).reshape(B, LANES, HEAD_DIM)

    row_head = (np.arange(nq) // qp) % H
    row_q = np.arange(nq) % qp
    same_head = row_head[:, None] == (np.arange(rpp) % H)[None, :]
    mask_page = np.where(same_head, 0.0, NEG_INF).astype(np.float32)
    new_key = np.arange(LANES) // H
    ok_new = np.logical_and(same_head[:, :LANES], new_key[None, :] <= row_q[:, None])
    mask_new = np.where(ok_new, 0.0, NEG_INF).astype(np.float32)

    def page_spec(p):
        return pl.BlockSpec((None, None, page, H, HEAD_DIM),
                            lambda b, c, pt: (layer, pt[b, (n_chunks - 1 - c) * pps + p], 0, 0, 0))

    per_seq = lambda r, cc: pl.BlockSpec((1, r, cc), lambda b, c, pt: (b, 0, 0))
    const2 = lambda shape: pl.BlockSpec(shape, lambda b, c, pt: (0, 0))
    in_specs = ([page_spec(p) for p in range(pps)] + [page_spec(p) for p in range(pps)]
                + [per_seq(nq, HEAD_DIM), per_seq(LANES, HEAD_DIM), per_seq(LANES, HEAD_DIM),
                   const2((nq, rpp)), const2((nq, LANES))])
    args = ([cache_k] * pps + [cache_v] * pps
            + [qh, new_rows(k_new), new_rows(v_new), jnp.asarray(mask_page), jnp.asarray(mask_new)])
    if kind == "diff":
        in_specs += [const2((nq, rpp)), const2((nq, LANES)), const2(dl.shape), const2((1, HEAD_DIM))]
        args += [bias_last, bias_new, dl, gain.reshape(1, HEAD_DIM)]
    else:
        in_specs += [pl.BlockSpec((1, 1, 1, pps * rpp), lambda b, c, pt: (b, n_chunks - 1 - c, 0, 0)),
                     per_seq(1, LANES)]
        f_new_rows = jnp.pad(f_new, ((0, 0), (0, SUBLANES - T), (0, 0))).reshape(B, 1, LANES)
        args += [f_past.reshape(B, n_chunks, 1, pps * rpp), f_new_rows]

    return pl.pallas_call(
        functools.partial(_attn_decode_kernel, kind=kind, pps=pps, n_new=T, qp=qp, lam_init=lam_init),
        grid_spec=pltpu.PrefetchScalarGridSpec(
            num_scalar_prefetch=1,
            grid=(B, n_chunks),
            in_specs=in_specs,
            out_specs=pl.BlockSpec((1, T, W), lambda b, c, pt: (b, 0, 0)),
            scratch_shapes=[pltpu.VMEM((pps * rpp, HEAD_DIM), BF16), pltpu.VMEM((pps * rpp, HEAD_DIM), BF16),
                            pltpu.VMEM((nq, 1), F32), pltpu.VMEM((nq, 1), F32),
                            pltpu.VMEM((nq, HEAD_DIM), F32)],
        ),
        out_shape=jax.ShapeDtypeStruct((B, T, W), F32),
        compiler_params=_params(2),
        name="attn_decode_" + kind,
    )(page_table, *args)


def _decode_bias(rel_bias, T, page):
    H = rel_bias.shape[1]
    far = rel_bias[N_BUCKETS - 1][:, None, None]
    last = _rel_bias_tiles(rel_bias, T, page, (page,))[:, 0] - far
    new = _rel_bias_tiles(rel_bias, T, SUBLANES, (0,))[:, 0] - far
    expand = lambda t: jnp.tile(jnp.repeat(t, H, axis=-1).reshape(H * T, -1), (2, 1))
    return expand(last), expand(new)


def _trunk(x, mod, past, wts, wf, bias_prompt, bias_decode):
    (ada_table, norm_attn, norm_mlp, norm_final, w_in, b_forget, diff_lambda, subln_gain,
     rel_bias, w_out, w_up, w_down) = wts
    B, T, D = x.shape
    depth = w_in.shape[0]
    H = rel_bias.shape[1]
    W = H * HEAD_DIM
    M = B * T
    prompt = past is None
    groups = B if prompt else 1
    c31 = rel_bias[N_BUCKETS - 1]

    def per_group(v):
        return v[:, None, :] if prompt else jnp.repeat(v, T, axis=0)[None]

    x = x.reshape(M, D)
    kv = [None] * 4
    rows = ([], [], [], [], [])
    for l in range(depth):
        m = mod + ada_table[l][None]
        shift_a, scale_a, gate_a, shift_m, scale_m, gate_m = [per_group(m[:, i]) for i in range(N_MOD)]
        lam_init = 0.8 - 0.6 * math.exp(-0.3 * l)
        h = _norm(x, norm_attn[l], scale_a, shift_a, groups=groups)
        proj = functools.partial(_matmul, h, w_in, l, w_rows="out")
        if prompt:
            qa = proj(0 * W, W, want_hm=True, scale=DH_HALF ** -0.5)
            qf = proj(3 * W, W, want_hm=True, scale=HEAD_DIM ** -0.5)
            hm = []
            for i, off in enumerate((1, 2, 4, 5)):
                kv[i], copy = proj(off * W, W, stack=(depth, kv[i]), want_hm=True)
                hm.append(copy)
            ka_hm, va_hm, kf_hm, vf_hm = hm
        else:
            qa, ka, va, qf, kf, vf = [proj(i * W, W, main_dtype=F32) for i in range(6)]
        logf, cum_t = _forget(h, wf, l, b_forget[l], T)
        if prompt:
            o = _attn_prompt("diff", qa, ka_hm, va_hm, B, T, bias=bias_prompt, c31=c31,
                             dl=diff_lambda[l], gain=subln_gain[l], lam_init=lam_init)
            o = _attn_prompt("fox", qf, kf_hm, vf_hm, B, T, fk=cum_t, out=o)
        else:
            page_table, c_ka, c_va, c_kf, c_vf, c_lf = past
            r4 = lambda a: a.reshape(B, T, H, HEAD_DIM)
            blast, bnew = bias_decode
            oa = _attn_decode("diff", l, page_table, c_ka, c_va, r4(qa), r4(ka), r4(va),
                              bias_last=blast, bias_new=bnew, dl=diff_lambda[l],
                              gain=subln_gain[l], lam_init=lam_init)
            of = _attn_decode("fox", l, page_table, c_kf, c_vf, r4(qf), r4(kf), r4(vf),
                              f_past=_f_past(l, page_table, c_lf), f_new=cum_t.T.reshape(B, T, H))
            o = jnp.concatenate([oa.reshape(M, W), of.reshape(M, W)], axis=-1).astype(BF16)
            for lst, r in zip(rows, (ka, va, kf, vf)):
                lst.append(r)
        rows[4].append(logf)
        x = _matmul(o, w_out, l, 0, D, epi="resid", gate=gate_a, resid=x, main_dtype=F32)
        h2 = _norm(x, norm_mlp[l], scale_m, shift_m, groups=groups)
        u = _matmul(h2, w_up, l, 0, w_up.shape[2], epi="relu2", main_dtype=BF16)
        x = _matmul(u, w_down, l, 0, D, groups=2 if prompt else 1, tm=512, tk=1024, epi="resid",
                    gate=gate_m, resid=x, main_dtype=F32)
    y = _norm(x, norm_final, out_dtype=F32).reshape(B, T, D)
    if prompt:
        kv = [t.reshape(depth, B, T, H, HEAD_DIM) for t in kv]
    else:
        kv = [jnp.stack(lst, axis=0).reshape(depth, B, T, H, HEAD_DIM) for lst in rows[:4]]
    return y, kv + [jnp.stack(rows[4], axis=0).reshape(depth, B, T, H)]


def kernel(x_prompt, x_sample, cache_k_diff, cache_v_diff, cache_k_fox, cache_v_fox, cache_logf_fox,
           page_table, c_prompt, c_sample, w_ada, b_ada, ada_table, norm_attn, norm_mlp, norm_final,
           w_in, b_forget, diff_lambda, subln_gain, rel_bias, w_out, w_up, w_down):
    w_in = jnp.swapaxes(w_in, 1, 2)
    wts = (ada_table, norm_attn, norm_mlp, norm_final, w_in, b_forget, diff_lambda, subln_gain,
           rel_bias, w_out, w_up, w_down)
    bp, _, D = x_prompt.shape
    bs, ts, _ = x_sample.shape
    H = rel_bias.shape[1]
    wf = jnp.pad(w_in[:, 6 * H * HEAD_DIM:, :], ((0, 0), (0, LANES - H), (0, 0)))

    c_all = jnp.concatenate([c_prompt, c_sample], axis=0)
    rows = -(-c_all.shape[0] // SUBLANES) * SUBLANES
    c_all = jnp.pad(c_all, ((0, rows - c_all.shape[0]), (0, 0)))
    mod = _matmul(c_all, w_ada[None], 0, 0, w_ada.shape[1], tm=rows, tn=512, silu=True, epi="bias",
                  bias=b_ada.reshape(1, -1), main_dtype=F32)
    mod = mod.reshape(rows, N_MOD, D)

    tq = _tile(x_prompt.shape[1], 256, LANES)
    bias_prompt = _rel_bias_tiles(rel_bias, tq, tq, (0, tq))
    y_prompt, rp = _trunk(x_prompt, mod[:bp], None, wts, wf, bias_prompt, None)

    past = (page_table, cache_k_diff, cache_v_diff, cache_k_fox, cache_v_fox, cache_logf_fox)
    bias_decode = _decode_bias(rel_bias, ts, cache_k_diff.shape[2])
    y_sample, rs = _trunk(x_sample, mod[bp:bp + bs], past, wts, wf, None, bias_decode)
    return (y_prompt, y_sample, rp[0], rp[1], rp[2], rp[3], rp[4], rs[0], rs[1], rs[2], rs[3], rs[4])
```

```python
import functools
import math

import numpy as np
import jax
import jax.numpy as jnp
from jax import lax
from jax.experimental import pallas as pl
from jax.experimental.pallas import tpu as pltpu

HEAD_DIM = 128
DH_HALF = HEAD_DIM // 2
N_BUCKETS = 32
MAX_DISTANCE = 128
EPS = 1e-6
N_MOD = 6
NEG_INF = -1e30
LANES = 128
SUBLANES = 8
VMEM_LIMIT = 56 * 1024 * 1024
PAGES_PER_STEP = 4
PAGES_PER_SUM_STEP = 16
F32 = jnp.float32
BF16 = jnp.bfloat16

_NT = (((1,), (1,)), ((), ()))


def _params(n_axes):
    return pltpu.CompilerParams(dimension_semantics=("arbitrary",) * n_axes,
                                vmem_limit_bytes=VMEM_LIMIT)


def _tile(dim, pref, align):
    t = (min(pref, dim) // align) * align
    while t >= align:
        if dim % t == 0:
            return t
        t -= align
    return dim


def _split3(x):
    hi = x.astype(BF16)
    r1 = x - hi.astype(F32)
    mid = r1.astype(BF16)
    lo = (r1 - mid.astype(F32)).astype(BF16)
    return hi, mid, lo


def _dot(a, b):
    return jnp.dot(a, b, preferred_element_type=F32)


def _dot_nt(a, b):
    return lax.dot_general(a, b, _NT, preferred_element_type=F32)


def _dot3_rhs(mat, x):
    hi, mid, lo = _split3(x)
    return _dot(mat, hi) + _dot(mat, mid) + _dot(mat, lo)


def _dot3_lhs(x, mat):
    hi, mid, lo = _split3(x)
    return _dot(hi, mat) + _dot(mid, mat) + _dot(lo, mat)


def _log_sigmoid(z):
    return jnp.minimum(z, 0.0) - jnp.log1p(jnp.exp(-jnp.abs(z)))


def _mm_kernel(*refs, nk, tm, silu, scale, epi, main, want_hm, aliased, w_rows):
    a_ref, w_ref = refs[0], refs[1]
    pos = 2
    bias_ref = gate_ref = resid_ref = main_ref = hm_ref = None
    if epi == "bias":
        bias_ref = refs[pos]
        pos += 1
    if epi == "resid":
        gate_ref, resid_ref = refs[pos], refs[pos + 1]
        pos += 2
    if aliased:
        pos += 1
    if main is not None:
        main_ref = refs[pos]
        pos += 1
    if want_hm:
        hm_ref = refs[pos]
        pos += 1
    wbf_ref = refs[pos]
    k = pl.program_id(2)
    m = pl.program_id(3)

    @pl.when(m == 0)
    def _():
        if w_rows == "in":
            wbf_ref[...] = w_ref[...].astype(BF16)
        else:
            tk = wbf_ref.shape[0]
            ck = _tile(tk, 512, LANES)
            for c in range(tk // ck):
                wbf_ref[c * ck:(c + 1) * ck, :] = w_ref[:, c * ck:(c + 1) * ck].T.astype(BF16)

    a = a_ref[...]
    if silu:
        a = (a * jax.nn.sigmoid(a)).astype(BF16)
    acc = _dot(a, wbf_ref[...])

    if nk == 1:
        if epi == "bias":
            acc = acc + bias_ref[...]
        elif epi == "relu2":
            r = jnp.maximum(acc, 0.0)
            acc = r * r
        elif epi == "resid":
            acc = resid_ref[...] + gate_ref[0] * acc
        if main == "flat":
            main_ref[...] = acc.astype(main_ref.dtype)
        elif main == "heads":
            for hh in range(main_ref.shape[1]):
                main_ref[:, hh, :] = acc[:, hh * HEAD_DIM:(hh + 1) * HEAD_DIM]
        if want_hm:
            for hh in range(hm_ref.shape[0]):
                piece = acc[:, hh * HEAD_DIM:(hh + 1) * HEAD_DIM]
                hm_ref[hh] = (piece * scale).astype(hm_ref.dtype)
    else:
        rows = pl.ds(pl.multiple_of(m * tm, tm), tm)

        @pl.when(k == 0)
        def _():
            main_ref[rows, :] = acc

        @pl.when(jnp.logical_and(k > 0, k < nk - 1))
        def _():
            main_ref[rows, :] += acc

        @pl.when(k == nk - 1)
        def _():
            main_ref[rows, :] = resid_ref[...] + gate_ref[0] * (main_ref[rows, :] + acc)


def _matmul(a, w, layer, n0, n, *, groups=1, tm=256, tn=1024, tk=None, silu=False, scale=1.0,
            epi="none", bias=None, gate=None, resid=None, main_dtype=None, stack=None,
            want_hm=False, w_rows="in"):
    M, K = a.shape
    mg = M // groups
    tm = _tile(mg, tm, SUBLANES)
    tn = _tile(n, tn, LANES)
    tk = K if tk is None else _tile(K, tk, LANES)
    assert n0 % tn == 0 and M % groups == 0
    nm, nj, nk = mg // tm, n // tn, K // tk
    jb0 = n0 // tn
    hpt = tn // HEAD_DIM
    main = "heads" if stack is not None else ("flat" if main_dtype is not None else None)
    assert nk == 1 or (epi == "resid" and main == "flat" and not want_hm)
    assert groups == 1 or nk > 1

    def row(g, m):
        return g * nm + m

    if w_rows == "in":
        w_spec = pl.BlockSpec((None, tk, tn), lambda g, j, k, m: (layer, k, jb0 + j))
    else:
        w_spec = pl.BlockSpec((None, tn, tk), lambda g, j, k, m: (layer, jb0 + j, k))
    in_specs = [pl.BlockSpec((tm, tk), lambda g, j, k, m: (row(g, m), k)), w_spec]
    args = [a, w]
    if epi == "bias":
        in_specs.append(pl.BlockSpec((1, tn), lambda g, j, k, m: (0, j)))
        args.append(bias)
    if epi == "resid":
        n_seq, rb, _ = gate.shape
        rows_per_seq = M // n_seq
        assert (rb == 1 and rows_per_seq % tm == 0) or (rb == tm and n_seq == 1 and M == tm)
        in_specs.append(pl.BlockSpec(
            (1, rb, tn), lambda g, j, k, m: ((row(g, m) * tm) // rows_per_seq, 0, j)))
        if nk == 1:
            in_specs.append(pl.BlockSpec((tm, tn), lambda g, j, k, m: (row(g, m), j)))
        else:
            in_specs.append(pl.BlockSpec(
                (tm, tn), lambda g, j, k, m: (g * nm + jnp.where(k == nk - 1, m, 0), j)))
        args += [gate, resid]

    out_shape, out_specs, aliases = [], [], {}
    if main == "heads":
        depth, buf = stack
        assert hpt % SUBLANES == 0 and nk == 1
        if buf is not None:
            aliases = {len(args): 0}
            in_specs.append(pl.BlockSpec(memory_space=pl.ANY))
            args.append(buf)
        out_shape.append(jax.ShapeDtypeStruct((depth, M, n // HEAD_DIM, HEAD_DIM), F32))
        out_specs.append(pl.BlockSpec((None, tm, hpt, HEAD_DIM),
                                      lambda g, j, k, m: (layer, row(g, m), j, 0)))
    elif main == "flat":
        out_shape.append(jax.ShapeDtypeStruct((M, n), main_dtype))
        if nk == 1:
            out_specs.append(pl.BlockSpec((tm, tn), lambda g, j, k, m: (row(g, m), j)))
        else:
            out_specs.append(pl.BlockSpec((mg, tn), lambda g, j, k, m: (g, j)))
    if want_hm:
        out_shape.append(jax.ShapeDtypeStruct((n // HEAD_DIM, M, HEAD_DIM), BF16))
        out_specs.append(pl.BlockSpec((hpt, tm, HEAD_DIM), lambda g, j, k, m: (j, row(g, m), 0)))

    kern = functools.partial(_mm_kernel, nk=nk, tm=tm, silu=silu, scale=scale, epi=epi,
                             main=main, want_hm=want_hm, aliased=bool(aliases), w_rows=w_rows)
    outs = pl.pallas_call(
        kern,
        grid=(groups, nj, nk, nm),
        in_specs=in_specs,
        out_specs=out_specs,
        out_shape=out_shape,
        scratch_shapes=[pltpu.VMEM((tk, tn), BF16)],
        input_output_aliases=aliases,
        compiler_params=_params(4),
        name="matmul_" + epi,
    )(*args)
    return outs[0] if len(outs) == 1 else tuple(outs)


def _norm_kernel(*refs, modulate):
    if modulate:
        x_ref, g_ref, sc_ref, sh_ref, o_ref = refs
    else:
        x_ref, g_ref, o_ref = refs
    x = x_ref[...]
    y = x * lax.rsqrt(jnp.mean(x * x, axis=-1, keepdims=True) + EPS) * g_ref[...]
    if modulate:
        y = y * (1.0 + sc_ref[0]) + sh_ref[0]
    o_ref[...] = y.astype(o_ref.dtype)


def _norm(x, g, scale=None, shift=None, *, groups=1, out_dtype=BF16, tm=256):
    M, D = x.shape
    mg = M // groups
    tm = _tile(mg, tm, SUBLANES)
    nm = mg // tm
    modulate = scale is not None
    in_specs = [pl.BlockSpec((tm, D), lambda i: (i, 0)), pl.BlockSpec((1, D), lambda i: (0, 0))]
    args = [x, g.reshape(1, D)]
    if modulate:
        rb = scale.shape[1]
        assert rb in (1, tm)
        spec = pl.BlockSpec((1, rb, D), lambda i: (i // nm, 0, 0))
        in_specs += [spec, spec]
        args += [scale, shift]
    return pl.pallas_call(
        functools.partial(_norm_kernel, modulate=modulate),
        grid=(M // tm,),
        in_specs=in_specs,
        out_specs=pl.BlockSpec((tm, D), lambda i: (i, 0)),
        out_shape=jax.ShapeDtypeStruct((M, D), out_dtype),
        compiler_params=_params(1),
        name="rmsnorm",
    )(*args)


def _forget_kernel(h_ref, wf_ref, b_ref, logf_ref, cumt_ref, carry_ref, *, tm, seq):
    i = pl.program_id(0)
    n_heads = logf_ref.shape[1]
    lf = _log_sigmoid(_dot_nt(h_ref[...], wf_ref[...].astype(BF16)) + b_ref[...])
    logf_ref[...] = lf[:, :n_heads]
    lt = lf.T[:n_heads]

    src = lax.broadcasted_iota(jnp.int32, (tm, tm), 0)
    dst = lax.broadcasted_iota(jnp.int32, (tm, tm), 1)
    tri = src <= dst
    if tm > seq:
        tri = jnp.logical_and(tri, src // seq == dst // seq)
    tri = jnp.where(tri, 1.0, 0.0).astype(BF16)
    cs = _dot3_lhs(lt, tri)
    if tm < seq:
        @pl.when((i * tm) % seq == 0)
        def _():
            carry_ref[...] = jnp.zeros_like(carry_ref)

        cs = cs + carry_ref[...]
        carry_ref[...] = cs[:, tm - 1:tm]
    cumt_ref[...] = cs


def _forget(h, wf, layer, b, seq):
    M, D = h.shape
    H = b.shape[0]
    rows = max(M, LANES)
    if rows != M:
        h = jnp.pad(h, ((0, rows - M), (0, 0)))
    tm = _tile(rows, 256, LANES)
    assert seq % tm == 0 or tm % seq == 0
    logf, cum_t = pl.pallas_call(
        functools.partial(_forget_kernel, tm=tm, seq=seq),
        grid=(rows // tm,),
        in_specs=[
            pl.BlockSpec((tm, D), lambda i: (i, 0)),
            pl.BlockSpec((None, LANES, D), lambda i: (layer, 0, 0)),
            pl.BlockSpec((1, LANES), lambda i: (0, 0)),
        ],
        out_specs=[pl.BlockSpec((tm, H), lambda i: (i, 0)), pl.BlockSpec((H, tm), lambda i: (0, i))],
        out_shape=[jax.ShapeDtypeStruct((rows, H), F32), jax.ShapeDtypeStruct((H, rows), F32)],
        scratch_shapes=[pltpu.VMEM((H, 1), F32)],
        compiler_params=_params(1),
        name="forget_gate",
    )(h, wf, jnp.pad(b, (0, LANES - H)).reshape(1, LANES))
    return logf[:M], cum_t[:, :M]


def _bucket_thresholds():
    n = np.arange(0, 4 * MAX_DISTANCE, dtype=np.int32)
    max_exact = N_BUCKETS // 2
    nf = np.maximum(n, 1).astype(np.float32)
    large = max_exact + (np.log(nf / np.float32(max_exact)) / np.float32(math.log(MAX_DISTANCE / max_exact))
                         * np.float32(N_BUCKETS - max_exact)).astype(np.int32)
    bucket = np.where(n < max_exact, n, np.minimum(large, N_BUCKETS - 1))
    assert np.all(np.diff(bucket) >= 0) and bucket[-1] == N_BUCKETS - 1
    return [int(np.argmax(bucket >= t)) for t in range(N_BUCKETS)]


def _rel_bias_kernel(rb_ref, o_ref, *, rel0s, thr):
    h = pl.program_id(0)
    _, _, R, C = o_ref.shape
    d = lax.broadcasted_iota(jnp.int32, (R, C), 0) - lax.broadcasted_iota(jnp.int32, (R, C), 1)
    for p, rel0 in enumerate(rel0s):
        rel = d + rel0
        acc = jnp.full((R, C), rb_ref[0, h], F32)
        for t in range(1, N_BUCKETS):
            acc = jnp.where(rel >= thr[t], rb_ref[t, h], acc)
        o_ref[0, p] = acc


def _rel_bias_tiles(rel_bias, R, C, rel0s):
    H = rel_bias.shape[1]
    return pl.pallas_call(
        functools.partial(_rel_bias_kernel, rel0s=tuple(rel0s), thr=_bucket_thresholds()),
        grid=(H,),
        in_specs=[pl.BlockSpec(memory_space=pltpu.SMEM)],
        out_specs=pl.BlockSpec((1, len(rel0s), R, C), lambda h: (h, 0, 0, 0)),
        out_shape=jax.ShapeDtypeStruct((H, len(rel0s), R, C), F32),
        compiler_params=_params(1),
        name="rel_bias_tiles",
    )(rel_bias)


def _lambda(dl_ref, lam_init):
    dl = dl_ref[...]
    s1 = jnp.sum(dl[0:1] * dl[1:2], axis=-1, keepdims=True)
    s2 = jnp.sum(dl[2:3] * dl[3:4], axis=-1, keepdims=True)
    return jnp.exp(s1) - jnp.exp(s2) + lam_init


def _subln(a, gain, lam_init):
    y = a * lax.rsqrt(jnp.mean(a * a, axis=-1, keepdims=True) + EPS) * gain
    return y * (1.0 - lam_init)


def _attn_prompt_kernel(*refs, kind, seq, tq, lam_init):
    if kind == "diff":
        c31_ref, q_ref, k_ref, v_ref, bias_ref, dl_ref, gain_ref, o_ref = refs
        c31 = c31_ref[pl.program_id(1)]
        d0 = bias_ref[0, 0]
        d1 = bias_ref[0, 1]
        d0 = jnp.concatenate([d0, d0], axis=0)
        d1 = jnp.concatenate([d1, d1], axis=0)
        lam = _lambda(dl_ref, lam_init)
        lane = lax.broadcasted_iota(jnp.int32, (tq, HEAD_DIM), 1)
    else:
        q_ref, k_ref, v_ref, fk_ref, _, o_ref = refs
        fk = fk_ref[0, 0]
    n_rows = 2 * tq if kind == "diff" else tq
    q_idx = lax.broadcasted_iota(jnp.int32, (n_rows, tq), 0)
    q_idx = jnp.where(q_idx >= tq, q_idx - tq, q_idx)
    vis = q_idx >= lax.broadcasted_iota(jnp.int32, (n_rows, tq), 1)

    for i in range(seq // tq):
        n = (i + 1) * tq
        q = q_ref[0, i * tq:n, :]
        if kind == "diff":
            zero = jnp.zeros_like(q)
            q = jnp.concatenate([jnp.where(lane < DH_HALF, q, zero),
                                 jnp.where(lane >= DH_HALF, q, zero)], axis=0)
        s = _dot_nt(q, k_ref[0, 0:n, :])
        parts = []
        if kind == "diff":
            if n > 2 * tq:
                parts.append(s[:, :n - 2 * tq] + c31)
            if n > tq:
                parts.append(s[:, n - 2 * tq:n - tq] + d1)
            parts.append(jnp.where(vis, s[:, n - tq:] + d0, NEG_INF))
        else:
            if n > tq:
                parts.append(s[:, :n - tq] - fk[:, :n - tq])
            parts.append(jnp.where(vis, s[:, n - tq:] - fk[:, n - tq:n], NEG_INF))
        mx = functools.reduce(jnp.maximum, [jnp.max(p, axis=-1, keepdims=True) for p in parts])
        parts = [jnp.exp(p - mx) for p in parts]
        den = functools.reduce(jnp.add, [jnp.sum(p, axis=-1, keepdims=True) for p in parts])
        pm = [p.astype(BF16) for p in parts]
        pm = pm[0] if len(pm) == 1 else jnp.concatenate(pm, axis=1)
        o = _dot(pm, v_ref[0, 0:n, :]) / den
        if kind == "diff":
            o = _subln(o[:tq] - lam * o[tq:], gain_ref[...], lam_init)
        o_ref[i * tq:n, :] = o.astype(o_ref.dtype)


def _attn_prompt(kind, q_hm, k_hm, v_hm, batch, seq, *, bias=None, c31=None, dl=None, gain=None,
                 fk=None, out=None, lam_init=0.0, tq=256):
    H, M, _ = q_hm.shape
    tq = _tile(seq, tq, LANES)
    hm_spec = pl.BlockSpec((1, seq, HEAD_DIM), lambda b, h: (h, b, 0))
    in_specs = [hm_spec, hm_spec, hm_spec]
    args = [q_hm, k_hm, v_hm]
    aliases = {}
    if kind == "diff":
        in_specs = [pl.BlockSpec(memory_space=pltpu.SMEM)] + in_specs + [
            pl.BlockSpec((1, 2, tq, tq), lambda b, h: (h, 0, 0, 0)),
            pl.BlockSpec(dl.shape, lambda b, h: (0, 0)),
            pl.BlockSpec((1, HEAD_DIM), lambda b, h: (0, 0)),
        ]
        args = [c31] + args + [bias, dl, gain.reshape(1, HEAD_DIM)]
        head0 = 0
    else:
        in_specs += [pl.BlockSpec((1, 1, 1, seq), lambda b, h: (h, b, 0, 0)),
                     pl.BlockSpec(memory_space=pl.ANY)]
        args += [fk.reshape(H, batch, 1, seq), out]
        aliases = {4: 0}
        head0 = H
    return pl.pallas_call(
        functools.partial(_attn_prompt_kernel, kind=kind, seq=seq, tq=tq, lam_init=lam_init),
        grid=(batch, H),
        in_specs=in_specs,
        out_specs=pl.BlockSpec((seq, HEAD_DIM), lambda b, h: (b, head0 + h)),
        out_shape=jax.ShapeDtypeStruct((M, 2 * H * HEAD_DIM), BF16),
        input_output_aliases=aliases,
        compiler_params=_params(2),
        name="attn_prompt_" + kind,
    )(*args)


def _f_past_kernel(*refs, pps):
    lf_pages = refs[1:1 + pps]
    o_ref, carry_ref = refs[1 + pps], refs[2 + pps]
    page = lf_pages[0].shape[0]

    @pl.when(pl.program_id(1) == 0)
    def _():
        carry_ref[...] = jnp.zeros_like(carry_ref)

    later = (lax.broadcasted_iota(jnp.int32, (page, page), 1)
             > lax.broadcasted_iota(jnp.int32, (page, page), 0))
    later = jnp.where(later, 1.0, 0.0).astype(BF16)
    carry = carry_ref[...]
    for p in reversed(range(pps)):
        lf = lf_pages[p][...]
        o_ref[0, p * page:(p + 1) * page, :] = -(_dot3_rhs(later, lf) + carry)
        carry = carry + jnp.sum(lf, axis=0, keepdims=True)
    carry_ref[...] = carry


def _f_past(layer, page_table, cache_lf):
    B, n_pages = page_table.shape
    _, _, page, H = cache_lf.shape
    pps = _tile(n_pages, PAGES_PER_SUM_STEP, 1)
    n_steps = n_pages // pps

    def page_spec(p):
        return pl.BlockSpec((None, None, page, H),
                            lambda b, c, pt: (layer, pt[b, (n_steps - 1 - c) * pps + p], 0, 0))

    return pl.pallas_call(
        functools.partial(_f_past_kernel, pps=pps),
        grid_spec=pltpu.PrefetchScalarGridSpec(
            num_scalar_prefetch=1,
            grid=(B, n_steps),
            in_specs=[page_spec(p) for p in range(pps)],
            out_specs=pl.BlockSpec((1, pps * page, H), lambda b, c, pt: (b, n_steps - 1 - c, 0)),
            scratch_shapes=[pltpu.VMEM((1, H), F32)],
        ),
        out_shape=jax.ShapeDtypeStruct((B, n_pages * page, H), F32),
        compiler_params=_params(2),
        name="f_past",
    )(page_table, *([cache_lf] * pps))


def _attn_decode_kernel(*refs, kind, pps, n_new, qp, lam_init):
    refs = list(refs)
    refs.pop(0)
    k_pages = [refs.pop(0) for _ in range(pps)]
    v_pages = [refs.pop(0) for _ in range(pps)]
    q_ref, knew_ref, vnew_ref, mpage_ref, mnew_ref = (refs.pop(0) for _ in range(5))
    if kind == "diff":
        blast_ref, bnew_ref, dl_ref, gain_ref = (refs.pop(0) for _ in range(4))
    else:
        fk_ref, fnew_ref = refs.pop(0), refs.pop(0)
    o_ref, kbuf, vbuf, m_ref, l_ref, acc_ref = refs
    c = pl.program_id(1)
    page, n_heads, _ = k_pages[0].shape
    rpp = page * n_heads
    q = q_ref[0]

    def update(parts, v_bf):
        m_old = m_ref[...]
        mx = functools.reduce(jnp.maximum, [jnp.max(p, axis=-1, keepdims=True) for p in parts])
        m_new = jnp.maximum(m_old, mx)
        alpha = jnp.exp(m_old - m_new)
        parts = [jnp.exp(p - m_new) for p in parts]
        den = functools.reduce(jnp.add, [jnp.sum(p, axis=-1, keepdims=True) for p in parts])
        pm = [p.astype(BF16) for p in parts]
        pm = pm[0] if len(pm) == 1 else jnp.concatenate(pm, axis=1)
        l_ref[...] = alpha * l_ref[...] + den
        acc_ref[...] = alpha * acc_ref[...] + _dot(pm, v_bf)
        m_ref[...] = m_new

    @pl.when(c == 0)
    def _():
        m_ref[...] = jnp.full_like(m_ref, NEG_INF)
        l_ref[...] = jnp.zeros_like(l_ref)
        acc_ref[...] = jnp.zeros_like(acc_ref)
        s = _dot_nt(q, knew_ref[0].astype(BF16)) + mnew_ref[...]
        if kind == "diff":
            s = s + bnew_ref[...]
        else:
            s = s - fnew_ref[0]
        update([s], vnew_ref[0].astype(BF16))

    for p in range(pps):
        kbuf[p * rpp:(p + 1) * rpp, :] = k_pages[p][...].reshape(rpp, HEAD_DIM).astype(BF16)
        vbuf[p * rpp:(p + 1) * rpp, :] = v_pages[p][...].reshape(rpp, HEAD_DIM).astype(BF16)
    s = _dot_nt(q, kbuf[...])
    parts = []
    for p in range(pps):
        sp = s[:, p * rpp:(p + 1) * rpp] + mpage_ref[...]
        if kind == "fox":
            sp = sp - fk_ref[0, 0, :, p * rpp:(p + 1) * rpp]
        elif p == pps - 1:
            sp = sp + jnp.where(c == 0, blast_ref[...], 0.0)
        parts.append(sp)
    update(parts, vbuf[...])

    @pl.when(c == pl.num_programs(1) - 1)
    def _():
        o = acc_ref[...] / l_ref[...]
        if kind == "diff":
            half = n_heads * qp
            o = _subln(o[:half] - _lambda(dl_ref, lam_init) * o[half:], gain_ref[...], lam_init)
        for h in range(n_heads):
            o_ref[0, :, h * HEAD_DIM:(h + 1) * HEAD_DIM] = o[h * qp:h * qp + n_new, :]


def _attn_decode(kind, layer, page_table, cache_k, cache_v, q, k_new, v_new, *, f_past=None,
                 f_new=None, bias_last=None, bias_new=None, dl=None, gain=None, lam_init=0.0):
    B, T, H, _ = q.shape
    W = H * HEAD_DIM
    page = cache_k.shape[2]
    n_pages = page_table.shape[1]
    pps = _tile(n_pages, PAGES_PER_STEP, 1)
    n_chunks = n_pages // pps
    rpp = page * H
    qp = T
    assert SUBLANES % T == 0 and page >= MAX_DISTANCE and SUBLANES * H == LANES
    n_maps = 2 if kind == "diff" else 1
    scale = (DH_HALF if kind == "diff" else HEAD_DIM) ** -0.5
    qh = jnp.transpose(q * scale, (0, 2, 1, 3)).reshape(B, H * qp, HEAD_DIM)
    if kind == "diff":
        lo = jnp.arange(HEAD_DIM) < DH_HALF
        qh = jnp.concatenate([jnp.where(lo, qh, 0.0), jnp.where(lo, 0.0, qh)], axis=1)
    qh = qh.astype(BF16)
    nq = n_maps * H * qp
    new_rows = lambda x: jnp.pad(x, ((0, 0), (0, SUBLANES - T), (0, 0), (0, 0))).reshape(B, LANES, HEAD_DIM)

    row_head = (np.arange(nq) // qp) % H
    row_q = np.arange(nq) % qp
    same_head = row_head[:, None] == (np.arange(rpp) % H)[None, :]
    mask_page = np.where(same_head, 0.0, NEG_INF).astype(np.float32)
    new_key = np.arange(LANES) // H
    ok_new = np.logical_and(same_head[:, :LANES], new_key[None, :] <= row_q[:, None])
    mask_new = np.where(ok_new, 0.0, NEG_INF).astype(np.float32)

    def page_spec(p):
        return pl.BlockSpec((None, None, page, H, HEAD_DIM),
                            lambda b, c, pt: (layer, pt[b, (n_chunks - 1 - c) * pps + p], 0, 0, 0))

    per_seq = lambda r, cc: pl.BlockSpec((1, r, cc), lambda b, c, pt: (b, 0, 0))
    const2 = lambda shape: pl.BlockSpec(shape, lambda b, c, pt: (0, 0))
    in_specs = ([page_spec(p) for p in range(pps)] + [page_spec(p) for p in range(pps)]
                + [per_seq(nq, HEAD_DIM), per_seq(LANES, HEAD_DIM), per_seq(LANES, HEAD_DIM),
                   const2((nq, rpp)), const2((nq, LANES))])
    args = ([cache_k] * pps + [cache_v] * pps
            + [qh, new_rows(k_new), new_rows(v_new), jnp.asarray(mask_page), jnp.asarray(mask_new)])
    if kind == "diff":
        in_specs += [const2((nq, rpp)), const2((nq, LANES)), const2(dl.shape), const2((1, HEAD_DIM))]
        args += [bias_last, bias_new, dl, gain.reshape(1, HEAD_DIM)]
    else:
        in_specs += [pl.BlockSpec((1, 1, 1, pps * rpp), lambda b, c, pt: (b, n_chunks - 1 - c, 0, 0)),
                     per_seq(1, LANES)]
        f_new_rows = jnp.pad(f_new, ((0, 0), (0, SUBLANES - T), (0, 0))).reshape(B, 1, LANES)
        args += [f_past.reshape(B, n_chunks, 1, pps * rpp), f_new_rows]

    return pl.pallas_call(
        functools.partial(_attn_decode_kernel, kind=kind, pps=pps, n_new=T, qp=qp, lam_init=lam_init),
        grid_spec=pltpu.PrefetchScalarGridSpec(
            num_scalar_prefetch=1,
            grid=(B, n_chunks),
            in_specs=in_specs,
            out_specs=pl.BlockSpec((1, T, W), lambda b, c, pt: (b, 0, 0)),
            scratch_shapes=[pltpu.VMEM((pps * rpp, HEAD_DIM), BF16), pltpu.VMEM((pps * rpp, HEAD_DIM), BF16),
                            pltpu.VMEM((nq, 1), F32), pltpu.VMEM((nq, 1), F32),
                            pltpu.VMEM((nq, HEAD_DIM), F32)],
        ),
        out_shape=jax.ShapeDtypeStruct((B, T, W), F32),
        compiler_params=_params(2),
        name="attn_decode_" + kind,
    )(page_table, *args)


def _decode_bias(rel_bias, T, page):
    H = rel_bias.shape[1]
    far = rel_bias[N_BUCKETS - 1][:, None, None]
    last = _rel_bias_tiles(rel_bias, T, page, (page,))[:, 0] - far
    new = _rel_bias_tiles(rel_bias, T, SUBLANES, (0,))[:, 0] - far
    expand = lambda t: jnp.tile(jnp.repeat(t, H, axis=-1).reshape(H * T, -1), (2, 1))
    return expand(last), expand(new)


def _trunk(x, mod, past, wts, wf, bias_prompt, bias_decode):
    (ada_table, norm_attn, norm_mlp, norm_final, w_in, b_forget, diff_lambda, subln_gain,
     rel_bias, w_out, w_up, w_down) = wts
    B, T, D = x.shape
    depth = w_in.shape[0]
    H = rel_bias.shape[1]
    W = H * HEAD_DIM
    M = B * T
    prompt = past is None
    groups = B if prompt else 1
    c31 = rel_bias[N_BUCKETS - 1]

    def per_group(v):
        return v[:, None, :] if prompt else jnp.repeat(v, T, axis=0)[None]

    x = x.reshape(M, D)
    kv = [None] * 4
    rows = ([], [], [], [], [])
    for l in range(depth):
        m = mod + ada_table[l][None]
        shift_a, scale_a, gate_a, shift_m, scale_m, gate_m = [per_group(m[:, i]) for i in range(N_MOD)]
        lam_init = 0.8 - 0.6 * math.exp(-0.3 * l)
        h = _norm(x, norm_attn[l], scale_a, shift_a, groups=groups)
        proj = functools.partial(_matmul, h, w_in, l, w_rows="out")
        if prompt:
            qa = proj(0 * W, W, want_hm=True, scale=DH_HALF ** -0.5)
            qf = proj(3 * W, W, want_hm=True, scale=HEAD_DIM ** -0.5)
            hm = []
            for i, off in enumerate((1, 2, 4, 5)):
                kv[i], copy = proj(off * W, W, stack=(depth, kv[i]), want_hm=True)
                hm.append(copy)
            ka_hm, va_hm, kf_hm, vf_hm = hm
        else:
            qa, ka, va, qf, kf, vf = [proj(i * W, W, main_dtype=F32) for i in range(6)]
        logf, cum_t = _forget(h, wf, l, b_forget[l], T)
        if prompt:
            o = _attn_prompt("diff", qa, ka_hm, va_hm, B, T, bias=bias_prompt, c31=c31,
                             dl=diff_lambda[l], gain=subln_gain[l], lam_init=lam_init)
            o = _attn_prompt("fox", qf, kf_hm, vf_hm, B, T, fk=cum_t, out=o)
        else:
            page_table, c_ka, c_va, c_kf, c_vf, c_lf = past
            r4 = lambda a: a.reshape(B, T, H, HEAD_DIM)
            blast, bnew = bias_decode
            oa = _attn_decode("diff", l, page_table, c_ka, c_va, r4(qa), r4(ka), r4(va),
                              bias_last=blast, bias_new=bnew, dl=diff_lambda[l],
                              gain=subln_gain[l], lam_init=lam_init)
            of = _attn_decode("fox", l, page_table, c_kf, c_vf, r4(qf), r4(kf), r4(vf),
                              f_past=_f_past(l, page_table, c_lf), f_new=cum_t.T.reshape(B, T, H))
            o = jnp.concatenate([oa.reshape(M, W), of.reshape(M, W)], axis=-1).astype(BF16)
            for lst, r in zip(rows, (ka, va, kf, vf)):
                lst.append(r)
        rows[4].append(logf)
        x = _matmul(o, w_out, l, 0, D, epi="resid", gate=gate_a, resid=x, main_dtype=F32)
        h2 = _norm(x, norm_mlp[l], scale_m, shift_m, groups=groups)
        u = _matmul(h2, w_up, l, 0, w_up.shape[2], epi="relu2", main_dtype=BF16)
        x = _matmul(u, w_down, l, 0, D, groups=groups, tm=512, tk=2048, epi="resid",
                    gate=gate_m, resid=x, main_dtype=F32)
    y = _norm(x, norm_final, out_dtype=F32).reshape(B, T, D)
    if prompt:
        kv = [t.reshape(depth, B, T, H, HEAD_DIM) for t in kv]
    else:
        kv = [jnp.stack(lst, axis=0).reshape(depth, B, T, H, HEAD_DIM) for lst in rows[:4]]
    return y, kv + [jnp.stack(rows[4], axis=0).reshape(depth, B, T, H)]


def kernel(x_prompt, x_sample, cache_k_diff, cache_v_diff, cache_k_fox, cache_v_fox, cache_logf_fox,
           page_table, c_prompt, c_sample, w_ada, b_ada, ada_table, norm_attn, norm_mlp, norm_final,
           w_in, b_forget, diff_lambda, subln_gain, rel_bias, w_out, w_up, w_down):
    w_in = jnp.swapaxes(w_in, 1, 2)
    wts = (ada_table, norm_attn, norm_mlp, norm_final, w_in, b_forget, diff_lambda, subln_gain,
           rel_bias, w_out, w_up, w_down)
    bp, _, D = x_prompt.shape
    bs, ts, _ = x_sample.shape
    H = rel_bias.shape[1]
    wf = jnp.pad(w_in[:, 6 * H * HEAD_DIM:, :], ((0, 0), (0, LANES - H), (0, 0)))

    c_all = jnp.concatenate([c_prompt, c_sample], axis=0)
    rows = -(-c_all.shape[0] // SUBLANES) * SUBLANES
    c_all = jnp.pad(c_all, ((0, rows - c_all.shape[0]), (0, 0)))
    mod = _matmul(c_all, w_ada[None], 0, 0, w_ada.shape[1], tm=rows, tn=512, silu=True, epi="bias",
                  bias=b_ada.reshape(1, -1), main_dtype=F32)
    mod = mod.reshape(rows, N_MOD, D)

    tq = _tile(x_prompt.shape[1], 256, LANES)
    bias_prompt = _rel_bias_tiles(rel_bias, tq, tq, (0, tq))
    y_prompt, rp = _trunk(x_prompt, mod[:bp], None, wts, wf, bias_prompt, None)

    past = (page_table, cache_k_diff, cache_v_diff, cache_k_fox, cache_v_fox, cache_logf_fox)
    bias_decode = _decode_bias(rel_bias, ts, cache_k_diff.shape[2])
    y_sample, rs = _trunk(x_sample, mod[bp:bp + bs], past, wts, wf, None, bias_decode)
    return (y_prompt, y_sample, rp[0], rp[1], rp[2], rp[3], rp[4], rs[0], rs[1], rs[2], rs[3], rs[4])
```
